```python
import math, functools
import jax, jax.numpy as jnp
from jax import lax
import numpy as np

D_MODEL = 1024
BATCH = 8
SEQ = 8192
DEPTH = 1
DEC_BATCH = 128
DEC_SEQ = 8
PAST_LEN = 8192
PAGE_SIZE = 128

HEAD_DIM = 64
MIX_WIDTH = D_MODEL
A_W = MIX_WIDTH // 2
B_W = MIX_WIDTH - A_W
H_ATTN = A_W // HEAD_DIM
H_RWKV = B_W // HEAD_DIM
BRANCHES = ((128, 1), (512, 4), (2048, 16))
MAX_WINDOW = 2048
MAX_DIL = 16
Q_BLOCK = 128
NUM_BUCKETS = 32
MAX_DISTANCE = 2048
LORA_W = 64
LORA_A = 64
LORA_G = 160
N_SHIFT = 3 * B_W + LORA_W + LORA_A + LORA_G
N_IN = 3 * A_W + N_SHIFT
D_FF = 4 * D_MODEL
RMS_EPS = 1e-6
GN_EPS = 64e-5
NEG_INF = -1e30

kernel_name = 'hybrid_dilated_attn_rwkv7_step'

f32 = jnp.float32


def rms_norm(x, g):
    xf = x.astype(f32)
    y = xf * lax.rsqrt(jnp.mean(xf * xf, -1, keepdims=True) + RMS_EPS) * g.astype(f32)
    return y.astype(x.dtype)


def t5_bucket(dist):
    dist = np.maximum(dist, 0)
    max_exact = NUM_BUCKETS // 2
    large = max_exact + (np.log(np.maximum(dist, 1) / max_exact)
                         / math.log(MAX_DISTANCE / max_exact)
                         * (NUM_BUCKETS - max_exact)).astype(np.int32)
    large = np.minimum(large, NUM_BUCKETS - 1)
    return np.where(dist < max_exact, dist, large).astype(np.int32)


def dilated_attention(q, k_ctx, v_ctx, q_start, bias_table, blk):
    B, Tq, H, E = q.shape
    pad = ((0, 0), (MAX_WINDOW, 0), (0, 0), (0, 0))
    kp = jnp.pad(k_ctx, pad)
    vp = jnp.pad(v_ctx, pad)
    scale = HEAD_DIM ** -0.5
    statics = []
    for w, d in BRANCHES:
        M, N = blk // d, (w + blk) // d
        m = np.arange(M)[:, None]
        n = np.arange(N)[None, :]
        band = jnp.asarray((n >= m) & (n <= m + w // d))
        bias = jnp.transpose(bias_table[t5_bucket(w + (m - n) * d)], (2, 0, 1)).astype(f32)
        statics.append((w, d, M, N, band, bias))

    def one_block(c0):
        qb = lax.dynamic_slice_in_dim(q, c0 - q_start, blk, axis=1)
        maxs, dens, outs = [], [], []
        for w, d, M, N, band, bias in statics:
            kr = lax.dynamic_slice_in_dim(kp, c0 + MAX_WINDOW - w, w + blk, axis=1).reshape(B, N, d, H, E)
            vr = lax.dynamic_slice_in_dim(vp, c0 + MAX_WINDOW - w, w + blk, axis=1).reshape(B, N, d, H, E)
            qr = qb.reshape(B, M, d, H, E)
            s = jnp.einsum('bmrhe,bnrhe->bhrmn', qr, kr).astype(f32) * scale + bias[None, :, None]
            valid = (c0 - w + jnp.arange(N)[None, :] * d + jnp.arange(d)[:, None]) >= 0
            s = jnp.where(band[None] & valid[:, None, :], s, NEG_INF)
            mx = jnp.max(s, -1)
            e = jnp.exp(s - mx[..., None])
            dn = jnp.sum(e, -1)
            o = jnp.einsum('bhrmn,bnrhe->bhrme', e, vr.astype(f32))
            maxs.append(mx.transpose(0, 3, 2, 1).reshape(B, blk, H))
            dens.append(dn.transpose(0, 3, 2, 1).reshape(B, blk, H))
            outs.append(o.transpose(0, 3, 2, 1, 4).reshape(B, blk, H, E))
        mall = maxs[0]
        for mx in maxs[1:]:
            mall = jnp.maximum(mall, mx)
        num = 0.0
        den = 0.0
        for mx, dn, o in zip(maxs, dens, outs):
            sc = jnp.exp(mx - mall)
            num = num + sc[..., None] * o
            den = den + sc * dn
        return num / den[..., None]

    c0s = q_start + blk * jnp.arange(Tq // blk, dtype=jnp.int32)
    out = lax.map(one_block, c0s)
    return out.transpose(1, 0, 2, 3, 4).reshape(B, Tq, H, E)


def wkv_scan(S0, r, w, k, v, kk, a):
    def step(S, inp):
        r_t, w_t, k_t, v_t, kk_t, a_t = inp
        sk = jnp.einsum('bhij,bhj->bhi', S, kk_t)
        S = (S * w_t[:, :, None, :] - sk[..., None] * (kk_t * a_t)[:, :, None, :]
             + v_t[..., None] * k_t[:, :, None, :])
        return S, jnp.einsum('bhij,bhj->bhi', S, r_t)
    xs = tuple(jnp.moveaxis(t, 1, 0) for t in (r, w, k, v, kk, a))
    S, y = lax.scan(step, S0, xs)
    return jnp.moveaxis(y, 0, 1), S


def rwkv7_time_mix(p, shift0, wkv0, mu_shift, w0, w_lora2, a0, a_lora2, g_lora2, k_k, k_a, r_k, lnx_g, lnx_b):
    B, T, _ = p.shape
    pf = p.astype(f32)
    prev = jnp.concatenate([shift0.astype(f32)[:, None], pf[:, :-1]], axis=1)
    xs = pf + (prev - pf) * mu_shift.astype(f32)
    r, k, v, xw, xa, xg = jnp.split(
        xs, (B_W, 2 * B_W, 3 * B_W, 3 * B_W + LORA_W, 3 * B_W + LORA_W + LORA_A), axis=-1)
    w_log = -jax.nn.softplus(-(w0.astype(f32) + jnp.tanh(xw) @ w_lora2.astype(f32))) - 0.5
    decay = jnp.exp(-jnp.exp(w_log))
    a = jax.nn.sigmoid(a0.astype(f32) + xa @ a_lora2.astype(f32))
    g = jax.nn.sigmoid(xg) @ g_lora2.astype(f32)
    heads = lambda t: t.reshape(B, T, H_RWKV, HEAD_DIM)
    kk = heads(k * k_k.astype(f32))
    kk = kk / jnp.maximum(jnp.sqrt(jnp.sum(kk * kk, -1, keepdims=True)), 1e-12)
    k = k * (1.0 + (a - 1.0) * k_a.astype(f32))
    r, k, v, decay, a = heads(r), heads(k), heads(v), heads(decay), heads(a)
    y, S = wkv_scan(wkv0.astype(f32), r, decay, k, v, kk, a)
    mean = jnp.mean(y, -1, keepdims=True)
    var = jnp.mean(jnp.square(y - mean), -1, keepdims=True)
    y = ((y - mean) * lax.rsqrt(var + GN_EPS)).reshape(B, T, B_W) * lnx_g.astype(f32) + lnx_b.astype(f32)
    bonus = jnp.sum(r * k * r_k.astype(f32), -1, keepdims=True) * v
    y = (y + bonus.reshape(B, T, B_W)) * g
    return y, S, p[:, -1]


def head_rms(x, g):
    xf = x.astype(f32)
    return (xf * lax.rsqrt(jnp.mean(xf * xf, -1, keepdims=True) + RMS_EPS) * g.astype(f32)).astype(x.dtype)


def decoder_layer(x, k_past, v_past, wkv0, shift0, bias_table, ln1_g, w_in, q_norm_g, k_norm_g,
                  mu_shift, w0, w_lora2, a0, a_lora2, g_lora2, k_k, k_a, r_k, lnx_g, lnx_b,
                  w_out, ln2_g, w_mlp1, w_mlp2):
    B, T, _ = x.shape
    n = rms_norm(x, ln1_g)
    proj = n @ w_in
    q = head_rms(proj[..., :A_W].reshape(B, T, H_ATTN, HEAD_DIM), q_norm_g)
    k = head_rms(proj[..., A_W:2 * A_W].reshape(B, T, H_ATTN, HEAD_DIM), k_norm_g)
    v = proj[..., 2 * A_W:3 * A_W].reshape(B, T, H_ATTN, HEAD_DIM)
    blk = min(Q_BLOCK, -(-T // MAX_DIL) * MAX_DIL)
    t_pad = -(-T // blk) * blk
    tp = ((0, 0), (0, t_pad - T), (0, 0), (0, 0))
    k_ctx = jnp.concatenate([k_past.astype(k.dtype), jnp.pad(k, tp)], axis=1)
    v_ctx = jnp.concatenate([v_past.astype(v.dtype), jnp.pad(v, tp)], axis=1)
    att = dilated_attention(jnp.pad(q, tp), k_ctx, v_ctx, k_past.shape[1], bias_table, blk)[:, :T]
    rw, wkv_T, shift_T = rwkv7_time_mix(proj[..., 3 * A_W:], shift0, wkv0, mu_shift, w0, w_lora2,
                                        a0, a_lora2, g_lora2, k_k, k_a, r_k, lnx_g, lnx_b)
    mixed = jnp.concatenate([att.reshape(B, T, A_W).astype(x.dtype), rw.astype(x.dtype)], axis=-1)
    h = x + mixed @ w_out
    m = rms_norm(h, ln2_g)
    y = h + jnp.square(jax.nn.relu(m @ w_mlp1)) @ w_mlp2
    return y, k, v, wkv_T, shift_T


def setup_inputs(seed: int = 0) -> dict:
    key = jax.random.key(seed)
    ks = jax.random.split(key, 32)
    L = DEPTH
    lw = min(MAX_WINDOW, PAST_LEN)
    nrm = lambda kk, shape, s: jax.random.normal(kk, shape, f32) * s
    return {
        'x_prompt': nrm(ks[0], (BATCH, SEQ, D_MODEL), 1.0),
        'x_sample': nrm(ks[1], (DEC_BATCH, DEC_SEQ, D_MODEL), 1.0),
        'cache_k_win': nrm(ks[2], (L, DEC_BATCH, lw, H_ATTN, HEAD_DIM), 1.0),
        'cache_v_win': nrm(ks[3], (L, DEC_BATCH, lw, H_ATTN, HEAD_DIM), 1.0),
        'state_wkv': nrm(ks[4], (L, DEC_BATCH, H_RWKV, HEAD_DIM, HEAD_DIM), 0.3),
        'state_shift': nrm(ks[5], (L, DEC_BATCH, N_SHIFT), 1.0),
        'bias_table': nrm(ks[6], (NUM_BUCKETS, H_ATTN), 0.5),
        'ln1_g': 1.0 + nrm(ks[7], (L, D_MODEL), 0.02),
        'w_in': nrm(ks[8], (L, D_MODEL, N_IN), D_MODEL ** -0.5),
        'q_norm_g': 1.0 + nrm(ks[9], (L, HEAD_DIM), 0.02),
        'k_norm_g': 1.0 + nrm(ks[10], (L, HEAD_DIM), 0.02),
        'mu_shift': jax.random.uniform(ks[11], (L, N_SHIFT), f32),
        'w0': jax.random.uniform(ks[12], (L, B_W), f32, -6.0, 1.0),
        'w_lora2': nrm(ks[13], (L, LORA_W, B_W), 0.1),
        'a0': nrm(ks[14], (L, B_W), 0.5),
        'a_lora2': nrm(ks[15], (L, LORA_A, B_W), 0.1),
        'g_lora2': nrm(ks[16], (L, LORA_G, B_W), LORA_G ** -0.5),
        'k_k': 0.85 + nrm(ks[17], (L, B_W), 0.05),
        'k_a': 1.0 + nrm(ks[18], (L, B_W), 0.05),
        'r_k': nrm(ks[19], (L, H_RWKV, HEAD_DIM), 0.1),
        'lnx_g': 1.0 + nrm(ks[20], (L, B_W), 0.02),
        'lnx_b': nrm(ks[21], (L, B_W), 0.02),
        'w_out': nrm(ks[22], (L, D_MODEL, D_MODEL), D_MODEL ** -0.5),
        'ln2_g': 1.0 + nrm(ks[23], (L, D_MODEL), 0.02),
        'w_mlp1': nrm(ks[24], (L, D_MODEL, D_FF), D_MODEL ** -0.5),
        'w_mlp2': nrm(ks[25], (L, D_FF, D_MODEL), D_FF ** -0.5),
    }


def reference(x_prompt, x_sample, cache_k_win, cache_v_win, state_wkv, state_shift, bias_table,
              ln1_g, w_in, q_norm_g, k_norm_g, mu_shift, w0, w_lora2, a0, a_lora2, g_lora2,
              k_k, k_a, r_k, lnx_g, lnx_b, w_out, ln2_g, w_mlp1, w_mlp2):
    yp, ys = x_prompt, x_sample
    Bp, Tp, _ = x_prompt.shape
    keep = min(MAX_WINDOW, Tp)
    kp_l, vp_l, sp_l, hp_l = [], [], [], []
    ks_l, vs_l, ss_l, hs_l = [], [], [], []
    for l in range(DEPTH):
        lw = (ln1_g[l], w_in[l], q_norm_g[l], k_norm_g[l], mu_shift[l], w0[l], w_lora2[l], a0[l],
              a_lora2[l], g_lora2[l], k_k[l], k_a[l], r_k[l], lnx_g[l], lnx_b[l], w_out[l], ln2_g[l],
              w_mlp1[l], w_mlp2[l])
        empty = jnp.zeros((Bp, 0, H_ATTN, HEAD_DIM), yp.dtype)
        yp, k_p, v_p, s_p, h_p = decoder_layer(
            yp, empty, empty, jnp.zeros((Bp, H_RWKV, HEAD_DIM, HEAD_DIM), f32),
            jnp.zeros((Bp, N_SHIFT), yp.dtype), bias_table, *lw)
        kp_l.append(k_p[:, Tp - keep:])
        vp_l.append(v_p[:, Tp - keep:])
        sp_l.append(s_p)
        hp_l.append(h_p)
        ys, k_s, v_s, s_s, h_s = decoder_layer(
            ys, cache_k_win[l], cache_v_win[l], state_wkv[l], state_shift[l], bias_table, *lw)
        ks_l.append(k_s)
        vs_l.append(v_s)
        ss_l.append(s_s)
        hs_l.append(h_s)
    return (yp, ys, jnp.stack(kp_l), jnp.stack(vp_l), jnp.stack(sp_l), jnp.stack(hp_l),
            jnp.stack(ks_l), jnp.stack(vs_l), jnp.stack(ss_l), jnp.stack(hs_l))
```

```python
import functools
import math

import numpy as np
import jax
import jax.numpy as jnp
from jax import lax
from jax.experimental import pallas as pl
from jax.experimental.pallas import tpu as pltpu

F32 = jnp.float32
BF16 = jnp.bfloat16

D_MODEL = 1024
HEAD_DIM = 64
A_W = 512
B_W = 512
H_ATTN = A_W // HEAD_DIM
H_RWKV = B_W // HEAD_DIM
BRANCHES = ((128, 1), (512, 4), (2048, 16))
MAX_WINDOW = 2048
WIN = 128
NUM_BUCKETS = 32
MAX_DISTANCE = 2048
LORA_W = 64
LORA_A = 64
LORA_G = 160
N_SHIFT = 3 * B_W + LORA_W + LORA_A + LORA_G
D_FF = 4 * D_MODEL
RMS_EPS = 1e-6
GN_EPS = 64e-5
NEG_INF = -1e30

LANES = 128
PAIR = 2 * HEAD_DIM
N_PAIRS = B_W // PAIR
P_PAD = 1920
LORA_PAD = P_PAD - 3 * B_W - LANES
VMEM_LIMIT = 56 * 1024 * 1024


def _dot(a, b):
    return jnp.dot(a.astype(BF16), b.astype(BF16), preferred_element_type=F32)


def _dot_nt(a, b):
    return lax.dot_general(a.astype(BF16), b.astype(BF16), (((1,), (1,)), ((), ())),
                           preferred_element_type=F32)


def _dot_tn(a, b):
    return lax.dot_general(a.astype(BF16), b.astype(BF16), (((0,), (0,)), ((), ())),
                           preferred_element_type=F32)


def _split2(x):
    hi = x.astype(BF16)
    lo = (x - hi.astype(F32)).astype(BF16)
    return hi, lo


def _split3(x):
    hi = x.astype(BF16)
    r1 = x - hi.astype(F32)
    mid = r1.astype(BF16)
    lo = (r1 - mid.astype(F32)).astype(BF16)
    return hi, mid, lo


def _group_reduce(x, bd):
    hi, mid, lo = _split3(x)
    return (jnp.dot(hi, bd, preferred_element_type=F32) + jnp.dot(mid, bd, preferred_element_type=F32)
            + jnp.dot(lo, bd, preferred_element_type=F32))


def _block_diag(n, value):
    idx = np.arange(n) // HEAD_DIM
    return jnp.asarray(np.where(idx[:, None] == idx[None, :], value, 0.0), BF16)


def _full(shape):
    return pl.BlockSpec(shape, lambda *_: (0,) * len(shape))


def _params(sem):
    return pltpu.CompilerParams(dimension_semantics=sem, vmem_limit_bytes=VMEM_LIMIT)


def _inproj_kernel(x_ref, g1_ref, wqkv_ref, wp_ref, gq_ref, gk_ref, bd_ref,
                   q_ref, k_ref, v_ref, kf_ref, vf_ref, p_ref):
    x = x_ref[...]
    ms = jnp.mean(x * x, axis=-1, keepdims=True)
    n = (x * lax.rsqrt(ms + RMS_EPS) * g1_ref[...]).astype(BF16)
    p_ref[...] = jnp.dot(n, wp_ref[...], preferred_element_type=F32)
    qkv = jnp.dot(n, wqkv_ref[...], preferred_element_type=F32)
    q = qkv[:, :A_W]
    k = qkv[:, A_W:2 * A_W]
    v = qkv[:, 2 * A_W:]
    bd = bd_ref[...]
    qn = q * lax.rsqrt(_group_reduce(q * q, bd) + RMS_EPS) * gq_ref[...]
    kn = k * lax.rsqrt(_group_reduce(k * k, bd) + RMS_EPS) * gk_ref[...]
    q_ref[...] = (qn * (HEAD_DIM ** -0.5)).astype(BF16)
    k_ref[...] = kn.astype(BF16)
    v_ref[...] = v.astype(BF16)
    kf_ref[...] = kn
    vf_ref[...] = v


def _inproj(x, g1, wqkv, wp, gq, gk, tm):
    n_tok = x.shape[0]
    tm = min(tm, n_tok)
    row = lambda w: pl.BlockSpec((tm, w), lambda i: (i, 0))
    return pl.pallas_call(
        _inproj_kernel,
        grid=(n_tok // tm,),
        in_specs=[row(D_MODEL), _full((1, D_MODEL)), _full((D_MODEL, 3 * A_W)), _full((D_MODEL, P_PAD)),
                  _full((1, A_W)), _full((1, A_W)), _full((A_W, A_W))],
        out_specs=[row(A_W), row(A_W), row(A_W), row(A_W), row(A_W), row(P_PAD)],
        out_shape=[jax.ShapeDtypeStruct((n_tok, A_W), BF16)] * 3
        + [jax.ShapeDtypeStruct((n_tok, A_W), F32)] * 2
        + [jax.ShapeDtypeStruct((n_tok, P_PAD), F32)],
        compiler_params=_params(("parallel",)),
        name="inproj",
    )(x, g1, wqkv, wp, gq, gk, _block_diag(A_W, 1.0 / HEAD_DIM))


def _head_masks():
    lane = lax.broadcasted_iota(jnp.int32, (1, PAIR), 1)
    return lane < HEAD_DIM, lane >= HEAD_DIM


def _attn_kernel(q_ref, kh_ref, kc_ref, vh_ref, vc_ref, bias_ref, o_ref, lse_ref, kbuf, vbuf, *, tq):
    i = pl.program_id(2)
    kbuf[0:WIN, :] = kh_ref[0]
    kbuf[WIN:, :] = kc_ref[0]
    vbuf[0:WIN, :] = vh_ref[0]
    vbuf[WIN:, :] = vc_ref[0]
    masks = _head_masks()

    def body(j, carry):
        off = pl.multiple_of(j * WIN, WIN)
        first = jnp.logical_and(i == 0, j == 0).astype(jnp.int32)
        for pr in range(N_PAIRS):
            sl = slice(pr * PAIR, (pr + 1) * PAIR)
            q2 = q_ref[0, pl.ds(off, WIN), sl]
            k2 = kbuf[pl.ds(off, 2 * WIN), sl]
            v2 = vbuf[pl.ds(off, 2 * WIN), sl]
            o_pair = None
            lse_pair = None
            for hh in range(2):
                qm = jnp.where(masks[hh], q2, jnp.zeros_like(q2))
                s = _dot_nt(qm, k2) + bias_ref[first, 2 * pr + hh]
                m = jnp.max(s, axis=-1, keepdims=True)
                e = jnp.exp(s - m)
                l = jnp.sum(e, axis=-1, keepdims=True)
                o = jnp.dot(e.astype(BF16), v2, preferred_element_type=F32) / l
                lse = jnp.broadcast_to(m + jnp.log(l), o.shape)
                o_pair = o if hh == 0 else jnp.where(masks[0], o_pair, o)
                lse_pair = lse if hh == 0 else jnp.where(masks[0], lse_pair, lse)
            o_ref[0, pl.ds(off, WIN), sl] = o_pair.astype(BF16)
            lse_ref[0, pl.ds(off, WIN), sl] = lse_pair
        return carry

    lax.fori_loop(0, tq // WIN, body, 0)


def _attn_branch(q, k, v, bias, d, tq):
    b, t, _ = q.shape
    l = t // d
    tq = min(tq, l)
    view = lambda a: a.reshape(b, l, d * A_W)
    sub = tq // WIN
    cur = pl.BlockSpec((1, tq, A_W), lambda bi, r, i: (bi, i, r))
    halo = pl.BlockSpec((1, WIN, A_W), lambda bi, r, i: (bi, jnp.maximum(i * sub - 1, 0), r))
    o, lse = pl.pallas_call(
        functools.partial(_attn_kernel, tq=tq),
        grid=(b, d, l // tq),
        in_specs=[cur, halo, cur, halo, cur, _full((2, H_ATTN, WIN, 2 * WIN))],
        out_specs=[cur, cur],
        out_shape=[jax.ShapeDtypeStruct((b, l, d * A_W), BF16), jax.ShapeDtypeStruct((b, l, d * A_W), F32)],
        scratch_shapes=[pltpu.VMEM((tq + WIN, A_W), BF16), pltpu.VMEM((tq + WIN, A_W), BF16)],
        compiler_params=_params(("parallel", "parallel", "arbitrary")),
        name=f"attn_d{d}",
    )(view(q), view(k), view(k), view(v), view(v), bias)
    return o.reshape(b * t, A_W), lse.reshape(b * t, A_W)


def _t5_bucket(dist):
    dist = np.maximum(dist, 0)
    max_exact = NUM_BUCKETS // 2
    large = max_exact + (np.log(np.maximum(dist, 1) / max_exact)
                         / math.log(MAX_DISTANCE / max_exact)
                         * (NUM_BUCKETS - max_exact)).astype(np.int32)
    large = np.minimum(large, NUM_BUCKETS - 1)
    return np.where(dist < max_exact, dist, large).astype(np.int32)


def _branch_bias(bias_table, d):
    m = np.arange(WIN)[:, None]
    n = np.arange(2 * WIN)[None, :]
    steps = m + WIN - n
    valid = (steps >= 0) & (steps <= WIN)
    bias = jnp.transpose(bias_table[_t5_bucket(steps * d)], (2, 0, 1)).astype(F32)
    normal = jnp.where(jnp.asarray(valid), bias, NEG_INF)
    first = jnp.where(jnp.asarray(valid & (n >= WIN)), bias, NEG_INF)
    return jnp.stack([normal, first])


def _sample_attn_kernel(q_ref, kn_ref, vn_ref, kc_ref, vc_ref, bias_ref, o_ref, kbuf, vbuf, *, lw, tq):
    kbuf[0:lw, :] = kc_ref[0].astype(BF16)
    vbuf[0:lw, :] = vc_ref[0].astype(BF16)
    pad = jnp.zeros((kbuf.shape[0] - lw, A_W), BF16)
    kbuf[lw:, :] = pad
    vbuf[lw:, :] = pad
    kbuf[lw:lw + tq, :] = kn_ref[0].astype(BF16)
    vbuf[lw:lw + tq, :] = vn_ref[0].astype(BF16)
    masks = _head_masks()
    q = q_ref[0].astype(BF16)
    for pr in range(N_PAIRS):
        sl = slice(pr * PAIR, (pr + 1) * PAIR)
        q2 = q[:, sl]
        k2 = kbuf[:, sl]
        v2 = vbuf[:, sl]
        o_pair = None
        for hh in range(2):
            qm = jnp.where(masks[hh], q2, jnp.zeros_like(q2))
            s = _dot_nt(qm, k2) + bias_ref[2 * pr + hh]
            m = jnp.max(s, axis=-1, keepdims=True)
            e = jnp.exp(s - m)
            l = jnp.sum(e, axis=-1, keepdims=True)
            o = jnp.dot(e.astype(BF16), v2, preferred_element_type=F32) / l
            o_pair = o if hh == 0 else jnp.where(masks[0], o_pair, o)
        o_ref[0, :, sl] = o_pair


def _sample_bias(bias_table, lw, tq, nbuf):
    i = np.arange(tq)[:, None]
    row = np.arange(nbuf)[None, :]
    dist = lw + i - row
    live = (dist >= 0) & (row < lw + tq)
    count = np.zeros(dist.shape, np.int32)
    for w, d in BRANCHES:
        count += (live & (dist <= w) & (dist % d == 0)).astype(np.int32)
    bias = jnp.transpose(bias_table[_t5_bucket(dist)], (2, 0, 1)).astype(F32)
    logc = jnp.log(jnp.asarray(np.maximum(count, 1)).astype(bias_table.dtype)).astype(F32)
    return jnp.where(jnp.asarray(count > 0), bias + logc, NEG_INF)


def _sample_attn(q, kn, vn, kc, vc, bias_table):
    b, tq, _ = q.shape
    lw = kc.shape[1]
    nbuf = lw + LANES
    new = pl.BlockSpec((1, tq, A_W), lambda bi: (bi, 0, 0))
    cache = pl.BlockSpec((1, lw, A_W), lambda bi: (bi, 0, 0))
    return pl.pallas_call(
        functools.partial(_sample_attn_kernel, lw=lw, tq=tq),
        grid=(b,),
        in_specs=[new, new, new, cache, cache, _full((H_ATTN, tq, nbuf))],
        out_specs=new,
        out_shape=jax.ShapeDtypeStruct((b, tq, A_W), F32),
        scratch_shapes=[pltpu.VMEM((nbuf, A_W), BF16), pltpu.VMEM((nbuf, A_W), BF16)],
        compiler_params=_params(("parallel",)),
        name="sample_attn",
    )(q, kn, vn, kc, vc, _sample_bias(bias_table, lw, tq, nbuf))


def _softplus(x):
    return jnp.maximum(x, 0.0) + jnp.log(1.0 + jnp.exp(-jnp.abs(x)))


def _rwkv_pair(s_bd, r, k, v, kk, a, cum, logw):
    c = r.shape[0]
    last = cum[c - 1:c, :]
    w_in = jnp.exp(cum)
    w_ex = jnp.exp(cum - logw)
    w_inv = jnp.exp(-cum)
    w_rem = jnp.exp(last - cum)
    b = kk * a
    rt = r * w_in
    at = -kk * w_ex
    rhs = jnp.concatenate([b * w_inv, k * w_inv], axis=0)
    masks = _head_masks()
    row = lax.broadcasted_iota(jnp.int32, (c, 2 * c), 0)
    col = lax.broadcasted_iota(jnp.int32, (c, 2 * c), 1)
    col = jnp.where(col >= c, col - c, col)
    zeros = jnp.zeros((c, PAIR), F32)
    p_sum = q_sum = g_sum = y_sum = None
    for hh in range(2):
        atm = jnp.where(masks[hh], at, 0.0)
        rtm = jnp.where(masks[hh], rt, 0.0)
        vm = jnp.where(masks[hh], v, 0.0)
        aa = _dot_nt(jnp.concatenate([atm, rtm], axis=0), rhs)
        a_a = jnp.where(col < row, aa[:c], 0.0)
        a_r = jnp.where(col <= row, aa[c:], 0.0)
        x = a_a[:, :c]
        tp = x
        n = c
        while n > 2:
            x = _dot(x, x)
            tp = tp + x + _dot(tp, x)
            n //= 2
        zv = jnp.concatenate([zeros, vm], axis=0)
        rhs2 = jnp.concatenate([atm, _dot(a_a, zv)], axis=1)
        pq = rhs2 + _dot(tp, rhs2)
        gy = _dot(a_r, jnp.concatenate([pq, jnp.concatenate([zeros, vm], axis=1)], axis=0))
        p_h, q_h, g_h, y_h = pq[:, :PAIR], pq[:, PAIR:], rtm + gy[:, :PAIR], gy[:, PAIR:]
        if hh == 0:
            p_sum, q_sum, g_sum, y_sum = p_h, q_h, g_h, y_h
        else:
            p_sum, q_sum, g_sum, y_sum = p_sum + p_h, q_sum + q_h, g_sum + g_h, y_sum + y_h
    uy = _dot_nt(jnp.concatenate([p_sum, g_sum], axis=0), s_bd)
    u = uy[:c] + q_sum
    y = uy[c:] + y_sum
    upd = _dot_tn(jnp.concatenate([u, v], axis=0), jnp.concatenate([b * w_rem, k * w_rem], axis=0))
    r_h = lax.broadcasted_iota(jnp.int32, (PAIR, PAIR), 0) >= HEAD_DIM
    c_h = lax.broadcasted_iota(jnp.int32, (PAIR, PAIR), 1) >= HEAD_DIM
    s_new = s_bd * jnp.exp(last) + jnp.where(r_h == c_h, upd, 0.0)
    return y, s_new


def _rwkv_kernel(p_ref, shift0_ref, s0_ref, mu_ref, w0_ref, wl_ref, a0_ref, al_ref, gl_ref,
                 kk_ref, ka_ref, rk_ref, lng_ref, lnb_ref, bd1_ref, bdm_ref,
                 y_ref, s_ref, carry):
    ci = pl.program_id(1)
    c = p_ref.shape[1]

    @pl.when(ci == 0)
    def _():
        carry[...] = shift0_ref[0]
        s_ref[0] = s0_ref[0]

    p = p_ref[0]
    row = lax.broadcasted_iota(jnp.int32, (c, 1), 0)
    prev = jnp.where(row == 0, carry[...], pltpu.roll(p, 1, 0))
    carry[...] = p[c - 1:c, :]
    xs = p + (prev - p) * mu_ref[...]
    r = xs[:, :B_W]
    k = xs[:, B_W:2 * B_W]
    v = xs[:, 2 * B_W:3 * B_W]
    xwa = xs[:, 3 * B_W:3 * B_W + LANES]
    xg = xs[:, 3 * B_W + LANES:]
    w_log = -_softplus(-(w0_ref[...] + _dot(jnp.tanh(xwa), wl_ref[...]))) - 0.5
    logw = -jnp.exp(w_log)
    a = jax.nn.sigmoid(a0_ref[...] + _dot(xwa, al_ref[...]))
    g = _dot(jax.nn.sigmoid(xg), gl_ref[...])
    bd1 = bd1_ref[...]
    kk = k * kk_ref[...]
    kk = kk / jnp.maximum(jnp.sqrt(_group_reduce(kk * kk, bd1)), 1e-12)
    k = k * (1.0 + (a - 1.0) * ka_ref[...])

    tri = (lax.broadcasted_iota(jnp.int32, (c, c), 1) <= lax.broadcasted_iota(jnp.int32, (c, c), 0))
    hi, mid, lo = _split3(logw)
    trib = tri.astype(BF16)
    cum = (jnp.dot(trib, hi, preferred_element_type=F32) + jnp.dot(trib, mid, preferred_element_type=F32)
           + jnp.dot(trib, lo, preferred_element_type=F32))

    ys = []
    for pr in range(N_PAIRS):
        sl = slice(pr * PAIR, (pr + 1) * PAIR)
        y, s_new = _rwkv_pair(s_ref[0, pr], r[:, sl], k[:, sl], v[:, sl], kk[:, sl], a[:, sl],
                              cum[:, sl], logw[:, sl])
        s_ref[0, pr] = s_new
        ys.append(y)
    y = jnp.concatenate(ys, axis=1)

    bdm = bdm_ref[...]
    dev = y - _group_reduce(y, bdm)
    yn = dev * lax.rsqrt(_group_reduce(dev * dev, bdm) + GN_EPS) * lng_ref[...] + lnb_ref[...]
    bonus = _group_reduce(r * k * rk_ref[...], bd1) * v
    y_ref[0] = ((yn + bonus) * g).astype(BF16)


def _rwkv(p, shift0, s0, weights, chunk):
    b, t, _ = p.shape
    wspecs = [_full(w.shape) for w in weights]
    return pl.pallas_call(
        _rwkv_kernel,
        grid=(b, t // chunk),
        in_specs=[pl.BlockSpec((1, chunk, P_PAD), lambda bi, ci: (bi, ci, 0)),
                  pl.BlockSpec((1, 1, P_PAD), lambda bi, ci: (bi, 0, 0)),
                  pl.BlockSpec((1, N_PAIRS, PAIR, PAIR), lambda bi, ci: (bi, 0, 0, 0))] + wspecs,
        out_specs=[pl.BlockSpec((1, chunk, B_W), lambda bi, ci: (bi, ci, 0)),
                   pl.BlockSpec((1, N_PAIRS, PAIR, PAIR), lambda bi, ci: (bi, 0, 0, 0))],
        out_shape=[jax.ShapeDtypeStruct((b, t, B_W), BF16),
                   jax.ShapeDtypeStruct((b, N_PAIRS, PAIR, PAIR), F32)],
        scratch_shapes=[pltpu.VMEM((1, P_PAD), F32)],
        compiler_params=_params(("parallel", "arbitrary")),
        name="rwkv7",
    )(p, shift0, s0, *weights)


def _pair_states(s):
    b = s.shape[0]
    s = s.reshape(b, N_PAIRS, 2, HEAD_DIM, HEAD_DIM)
    z = jnp.zeros_like(s[:, :, 0])
    top = jnp.concatenate([s[:, :, 0], z], axis=-1)
    bot = jnp.concatenate([z, s[:, :, 1]], axis=-1)
    return jnp.concatenate([top, bot], axis=-2)


def _unpair_states(s):
    return jnp.stack([s[:, :, :HEAD_DIM, :HEAD_DIM], s[:, :, HEAD_DIM:, HEAD_DIM:]], axis=2).reshape(
        s.shape[0], H_RWKV, HEAD_DIM, HEAD_DIM)


def _outmlp_kernel(*refs, n_branch):
    x_ref = refs[0]
    o_refs = refs[1:1 + n_branch]
    lse_refs = refs[1 + n_branch:1 + 2 * n_branch] if n_branch > 1 else ()
    rw_ref, wo_ref, g2_ref, w1_ref, w2_ref, y_ref = refs[1 + n_branch + len(lse_refs):]
    if n_branch > 1:
        lses = [r[...] for r in lse_refs]
        top = functools.reduce(jnp.maximum, lses)
        num = den = None
        for o_ref, lse in zip(o_refs, lses):
            e = jnp.exp(lse - top)
            t = e * o_ref[...].astype(F32)
            num = t if num is None else num + t
            den = e if den is None else den + e
        att = num / den
    else:
        att = o_refs[0][...]
    h = x_ref[...] + _dot(att, wo_ref[:A_W, :]) + jnp.dot(rw_ref[...], wo_ref[A_W:, :],
                                                           preferred_element_type=F32)
    ms = jnp.mean(h * h, axis=-1, keepdims=True)
    m = (h * lax.rsqrt(ms + RMS_EPS) * g2_ref[...]).astype(BF16)
    acc = None
    for c in range(D_FF // D_MODEL):
        sl = slice(c * D_MODEL, (c + 1) * D_MODEL)
        u = jnp.maximum(jnp.dot(m, w1_ref[:, sl], preferred_element_type=F32), 0.0)
        t = jnp.dot((u * u).astype(BF16), w2_ref[sl, :], preferred_element_type=F32)
        acc = t if acc is None else acc + t
    y_ref[...] = h + acc


def _outmlp(x, os_, lses, rw, wo, g2, w1, w2, tm):
    n_tok = x.shape[0]
    tm = min(tm, n_tok)
    n_branch = len(os_)
    row = lambda w: pl.BlockSpec((tm, w), lambda i: (i, 0))
    const = lambda shape: pl.BlockSpec(shape, lambda i: (0, 0), pipeline_mode=pl.Buffered(1))
    return pl.pallas_call(
        functools.partial(_outmlp_kernel, n_branch=n_branch),
        grid=(n_tok // tm,),
        in_specs=[row(D_MODEL)] + [row(A_W)] * (n_branch + len(lses)) + [row(B_W)]
        + [const((D_MODEL, D_MODEL)), const((1, D_MODEL)), const((D_MODEL, D_FF)), const((D_FF, D_MODEL))],
        out_specs=row(D_MODEL),
        out_shape=jax.ShapeDtypeStruct((n_tok, D_MODEL), F32),
        compiler_params=_params(("parallel",)),
        name="outmlp",
    )(x, *os_, *lses, rw, wo, g2, w1, w2)


def _row(v, width=None):
    v = v.reshape(1, -1).astype(F32)
    if width is not None and v.shape[1] < width:
        v = jnp.pad(v, ((0, 0), (0, width - v.shape[1])))
    return v


def _tile_heads(g, n_heads):
    return jnp.tile(g.reshape(1, HEAD_DIM).astype(F32), (1, n_heads))


def kernel(x_prompt, x_sample, cache_k_win, cache_v_win, state_wkv, state_shift, bias_table, ln1_g, w_in,
           q_norm_g, k_norm_g, mu_shift, w0, w_lora2, a0, a_lora2, g_lora2, k_k, k_a, r_k, lnx_g, lnx_b,
           w_out, ln2_g, w_mlp1, w_mlp2):
    depth = w_in.shape[0]
    assert depth == 1, "a deeper stack would repeat the per-layer calls below"
    li = 0
    bp, tp, _ = x_prompt.shape
    bs, ts, _ = x_sample.shape
    keep = min(MAX_WINDOW, tp)

    wqkv = w_in[li][:, :3 * A_W].astype(BF16)
    wp = jnp.pad(w_in[li][:, 3 * A_W:], ((0, 0), (0, P_PAD - N_SHIFT))).astype(BF16)
    gq = _tile_heads(q_norm_g[li], H_ATTN)
    gk = _tile_heads(k_norm_g[li], H_ATTN)
    zrow = lambda n: jnp.zeros((n, B_W), F32)
    rwkv_weights = [
        _row(mu_shift[li], P_PAD), _row(w0[li]),
        jnp.concatenate([w_lora2[li].astype(F32), zrow(LANES - LORA_W)], axis=0).astype(BF16),
        _row(a0[li]),
        jnp.concatenate([zrow(LORA_W), a_lora2[li].astype(F32)], axis=0).astype(BF16),
        jnp.concatenate([g_lora2[li].astype(F32), zrow(LORA_PAD - LORA_G)], axis=0).astype(BF16),
        _row(k_k[li]), _row(k_a[li]), _row(r_k[li]), _row(lnx_g[li]), _row(lnx_b[li]),
        _block_diag(B_W, 1.0), _block_diag(B_W, 1.0 / HEAD_DIM),
    ]
    wo = w_out[li].astype(BF16)
    w1 = w_mlp1[li].astype(BF16)
    w2 = w_mlp2[li].astype(BF16)
    g1 = _row(ln1_g[li])
    g2 = _row(ln2_g[li])

    xp = x_prompt.reshape(bp * tp, D_MODEL)
    q, k, v, kf, vf, p = _inproj(xp, g1, wqkv, wp, gq, gk, tm=512)
    q3, k3, v3 = (a.reshape(bp, tp, A_W) for a in (q, k, v))
    os_, lses = [], []
    for w, d in BRANCHES:
        o, lse = _attn_branch(q3, k3, v3, _branch_bias(bias_table, d), d, tq=1024)
        os_.append(o)
        lses.append(lse)
    p3 = p.reshape(bp, tp, P_PAD)
    rw, s_p = _rwkv(p3, jnp.zeros((bp, 1, P_PAD), F32), jnp.zeros((bp, N_PAIRS, PAIR, PAIR), F32),
                    rwkv_weights, chunk=64)
    y_p = _outmlp(xp, os_, lses, rw.reshape(bp * tp, B_W), wo, g2, w1, w2, tm=256)
    heads = lambda a, b_, t_: a.reshape(b_, t_, H_ATTN, HEAD_DIM)
    k_win_p = heads(kf, bp, tp)[:, tp - keep:][None]
    v_win_p = heads(vf, bp, tp)[:, tp - keep:][None]
    wkv_p = _unpair_states(s_p)[None]
    shift_p = p3[:, -1, :N_SHIFT][None]

    xs = x_sample.reshape(bs * ts, D_MODEL)
    qs, _, _, kfs, vfs, ps = _inproj(xs, g1, wqkv, wp, gq, gk, tm=512)
    lw = cache_k_win.shape[2]
    att_s = _sample_attn(qs.reshape(bs, ts, A_W).astype(F32), kfs.reshape(bs, ts, A_W), vfs.reshape(bs, ts, A_W),
                         cache_k_win[li].reshape(bs, lw, A_W), cache_v_win[li].reshape(bs, lw, A_W), bias_table)
    ps3 = ps.reshape(bs, ts, P_PAD)
    shift0 = jnp.pad(state_shift[li].astype(F32), ((0, 0), (0, P_PAD - N_SHIFT))).reshape(bs, 1, P_PAD)
    rw_s, s_s = _rwkv(ps3, shift0, _pair_states(state_wkv[li].astype(F32)), rwkv_weights, chunk=ts)
    y_s = _outmlp(xs, [att_s.reshape(bs * ts, A_W)], [], rw_s.reshape(bs * ts, B_W), wo, g2, w1, w2, tm=256)

    return (y_p.reshape(bp, tp, D_MODEL), y_s.reshape(bs, ts, D_MODEL), k_win_p, v_win_p, wkv_p, shift_p,
            heads(kfs, bs, ts)[None], heads(vfs, bs, ts)[None], _unpair_states(s_s)[None],
            ps3[:, -1, :N_SHIFT][None])
```

```python
import functools
import math

import numpy as np
import jax
import jax.numpy as jnp
from jax import lax
from jax.experimental import pallas as pl
from jax.experimental.pallas import tpu as pltpu

F32 = jnp.float32
BF16 = jnp.bfloat16

D_MODEL = 1024
HEAD_DIM = 64
A_W = 512
B_W = 512
H_ATTN = A_W // HEAD_DIM
H_RWKV = B_W // HEAD_DIM
BRANCHES = ((128, 1), (512, 4), (2048, 16))
MAX_WINDOW = 2048
WIN = 128
NUM_BUCKETS = 32
MAX_DISTANCE = 2048
LORA_W = 64
LORA_A = 64
LORA_G = 160
N_SHIFT = 3 * B_W + LORA_W + LORA_A + LORA_G
D_FF = 4 * D_MODEL
RMS_EPS = 1e-6
GN_EPS = 64e-5
NEG_INF = -1e30

LANES = 128
PAIR = 2 * HEAD_DIM
N_PAIRS = B_W // PAIR
P_PAD = 1920
LORA_PAD = P_PAD - 3 * B_W - LANES
VMEM_LIMIT = 56 * 1024 * 1024


def _dot(a, b):
    return jnp.dot(a.astype(BF16), b.astype(BF16), preferred_element_type=F32)


def _dot_nt(a, b):
    return lax.dot_general(a.astype(BF16), b.astype(BF16), (((1,), (1,)), ((), ())),
                           preferred_element_type=F32)


def _dot_tn(a, b):
    return lax.dot_general(a.astype(BF16), b.astype(BF16), (((0,), (0,)), ((), ())),
                           preferred_element_type=F32)


def _split2(x):
    hi = x.astype(BF16)
    lo = (x - hi.astype(F32)).astype(BF16)
    return hi, lo


def _split3(x):
    hi = x.astype(BF16)
    r1 = x - hi.astype(F32)
    mid = r1.astype(BF16)
    lo = (r1 - mid.astype(F32)).astype(BF16)
    return hi, mid, lo


def _group_reduce(x, bd):
    hi, mid, lo = _split3(x)
    return (jnp.dot(hi, bd, preferred_element_type=F32) + jnp.dot(mid, bd, preferred_element_type=F32)
            + jnp.dot(lo, bd, preferred_element_type=F32))


def _block_diag(n, value):
    idx = np.arange(n) // HEAD_DIM
    return jnp.asarray(np.where(idx[:, None] == idx[None, :], value, 0.0), BF16)


def _full(shape):
    return pl.BlockSpec(shape, lambda *_: (0,) * len(shape))


def _params(sem):
    return pltpu.CompilerParams(dimension_semantics=sem, vmem_limit_bytes=VMEM_LIMIT)


def _inproj_kernel(x_ref, g1_ref, wqkv_ref, wp_ref, gq_ref, gk_ref, bd_ref,
                   q_ref, k_ref, v_ref, kf_ref, vf_ref, p_ref):
    x = x_ref[...]
    ms = jnp.mean(x * x, axis=-1, keepdims=True)
    n = (x * lax.rsqrt(ms + RMS_EPS) * g1_ref[...]).astype(BF16)
    p_ref[...] = jnp.dot(n, wp_ref[...], preferred_element_type=F32)
    qkv = jnp.dot(n, wqkv_ref[...], preferred_element_type=F32)
    q = qkv[:, :A_W]
    k = qkv[:, A_W:2 * A_W]
    v = qkv[:, 2 * A_W:]
    bd = bd_ref[...]
    qn = q * lax.rsqrt(_group_reduce(q * q, bd) + RMS_EPS) * gq_ref[...]
    kn = k * lax.rsqrt(_group_reduce(k * k, bd) + RMS_EPS) * gk_ref[...]
    q_ref[...] = (qn * (HEAD_DIM ** -0.5)).astype(BF16)
    k_ref[...] = kn.astype(BF16)
    v_ref[...] = v.astype(BF16)
    kf_ref[...] = kn
    vf_ref[...] = v


def _inproj(x, g1, wqkv, wp, gq, gk, tm):
    n_tok = x.shape[0]
    tm = min(tm, n_tok)
    row = lambda w: pl.BlockSpec((tm, w), lambda i: (i, 0))
    return pl.pallas_call(
        _inproj_kernel,
        grid=(n_tok // tm,),
        in_specs=[row(D_MODEL), _full((1, D_MODEL)), _full((D_MODEL, 3 * A_W)), _full((D_MODEL, P_PAD)),
                  _full((1, A_W)), _full((1, A_W)), _full((A_W, A_W))],
        out_specs=[row(A_W), row(A_W), row(A_W), row(A_W), row(A_W), row(P_PAD)],
        out_shape=[jax.ShapeDtypeStruct((n_tok, A_W), BF16)] * 3
        + [jax.ShapeDtypeStruct((n_tok, A_W), F32)] * 2
        + [jax.ShapeDtypeStruct((n_tok, P_PAD), F32)],
        compiler_params=_params(("parallel",)),
        name="inproj",
    )(x, g1, wqkv, wp, gq, gk, _block_diag(A_W, 1.0 / HEAD_DIM))


def _head_masks():
    lane = lax.broadcasted_iota(jnp.int32, (1, PAIR), 1)
    return lane < HEAD_DIM, lane >= HEAD_DIM


def _attn_kernel(q_ref, kh_ref, kc_ref, vh_ref, vc_ref, bias_ref, o_ref, lse_ref, kbuf, vbuf, *, tq):
    i = pl.program_id(2)
    kbuf[0:WIN, :] = kh_ref[0]
    kbuf[WIN:, :] = kc_ref[0]
    vbuf[0:WIN, :] = vh_ref[0]
    vbuf[WIN:, :] = vc_ref[0]
    masks = _head_masks()

    def body(j, carry):
        off = pl.multiple_of(j * WIN, WIN)
        first = jnp.logical_and(i == 0, j == 0).astype(jnp.int32)
        for pr in range(N_PAIRS):
            sl = slice(pr * PAIR, (pr + 1) * PAIR)
            q2 = q_ref[0, pl.ds(off, WIN), sl]
            k2 = kbuf[pl.ds(off, 2 * WIN), sl]
            v2 = vbuf[pl.ds(off, 2 * WIN), sl]
            o_pair = None
            lse_pair = None
            for hh in range(2):
                qm = jnp.where(masks[hh], q2, jnp.zeros_like(q2))
                s = _dot_nt(qm, k2) + bias_ref[first, 2 * pr + hh]
                m = jnp.max(s, axis=-1, keepdims=True)
                e = jnp.exp(s - m)
                l = jnp.sum(e, axis=-1, keepdims=True)
                o = jnp.dot(e.astype(BF16), v2, preferred_element_type=F32) / l
                lse = jnp.broadcast_to(m + jnp.log(l), o.shape)
                o_pair = o if hh == 0 else jnp.where(masks[0], o_pair, o)
                lse_pair = lse if hh == 0 else jnp.where(masks[0], lse_pair, lse)
            o_ref[0, pl.ds(off, WIN), sl] = o_pair.astype(BF16)
            lse_ref[0, pl.ds(off, WIN), sl] = lse_pair
        return carry

    lax.fori_loop(0, tq // WIN, body, 0)


def _attn_branch(q, k, v, bias, d, tq):
    b, t, _ = q.shape
    l = t // d
    tq = min(tq, l)
    view = lambda a: a.reshape(b, l, d * A_W)
    sub = tq // WIN
    cur = pl.BlockSpec((1, tq, A_W), lambda bi, r, i: (bi, i, r))
    halo = pl.BlockSpec((1, WIN, A_W), lambda bi, r, i: (bi, jnp.maximum(i * sub - 1, 0), r))
    o, lse = pl.pallas_call(
        functools.partial(_attn_kernel, tq=tq),
        grid=(b, d, l // tq),
        in_specs=[cur, halo, cur, halo, cur, _full((2, H_ATTN, WIN, 2 * WIN))],
        out_specs=[cur, cur],
        out_shape=[jax.ShapeDtypeStruct((b, l, d * A_W), BF16), jax.ShapeDtypeStruct((b, l, d * A_W), F32)],
        scratch_shapes=[pltpu.VMEM((tq + WIN, A_W), BF16), pltpu.VMEM((tq + WIN, A_W), BF16)],
        compiler_params=_params(("parallel", "parallel", "arbitrary")),
        name=f"attn_d{d}",
    )(view(q), view(k), view(k), view(v), view(v), bias)
    return o.reshape(b * t, A_W), lse.reshape(b * t, A_W)


def _t5_bucket(dist):
    dist = np.maximum(dist, 0)
    max_exact = NUM_BUCKETS // 2
    large = max_exact + (np.log(np.maximum(dist, 1) / max_exact)
                         / math.log(MAX_DISTANCE / max_exact)
                         * (NUM_BUCKETS - max_exact)).astype(np.int32)
    large = np.minimum(large, NUM_BUCKETS - 1)
    return np.where(dist < max_exact, dist, large).astype(np.int32)


def _branch_bias(bias_table, d):
    m = np.arange(WIN)[:, None]
    n = np.arange(2 * WIN)[None, :]
    steps = m + WIN - n
    valid = (steps >= 0) & (steps <= WIN)
    bias = jnp.transpose(bias_table[_t5_bucket(steps * d)], (2, 0, 1)).astype(F32)
    normal = jnp.where(jnp.asarray(valid), bias, NEG_INF)
    first = jnp.where(jnp.asarray(valid & (n >= WIN)), bias, NEG_INF)
    return jnp.stack([normal, first])


def _sample_attn_kernel(q_ref, kn_ref, vn_ref, kc_ref, vc_ref, bias_ref, o_ref, kbuf, vbuf, *, lw, tq):
    kbuf[0:lw, :] = kc_ref[0].astype(BF16)
    vbuf[0:lw, :] = vc_ref[0].astype(BF16)
    pad = jnp.zeros((kbuf.shape[0] - lw, A_W), BF16)
    kbuf[lw:, :] = pad
    vbuf[lw:, :] = pad
    kbuf[lw:lw + tq, :] = kn_ref[0].astype(BF16)
    vbuf[lw:lw + tq, :] = vn_ref[0].astype(BF16)
    masks = _head_masks()
    q = q_ref[0].astype(BF16)
    for pr in range(N_PAIRS):
        sl = slice(pr * PAIR, (pr + 1) * PAIR)
        q2 = q[:, sl]
        k2 = kbuf[:, sl]
        v2 = vbuf[:, sl]
        o_pair = None
        for hh in range(2):
            qm = jnp.where(masks[hh], q2, jnp.zeros_like(q2))
            s = _dot_nt(qm, k2) + bias_ref[2 * pr + hh]
            m = jnp.max(s, axis=-1, keepdims=True)
            e = jnp.exp(s - m)
            l = jnp.sum(e, axis=-1, keepdims=True)
            o = jnp.dot(e.astype(BF16), v2, preferred_element_type=F32) / l
            o_pair = o if hh == 0 else jnp.where(masks[0], o_pair, o)
        o_ref[0, :, sl] = o_pair


def _sample_bias(bias_table, lw, tq, nbuf):
    i = np.arange(tq)[:, None]
    row = np.arange(nbuf)[None, :]
    dist = lw + i - row
    live = (dist >= 0) & (row < lw + tq)
    count = np.zeros(dist.shape, np.int32)
    for w, d in BRANCHES:
        count += (live & (dist <= w) & (dist % d == 0)).astype(np.int32)
    bias = jnp.transpose(bias_table[_t5_bucket(dist)], (2, 0, 1)).astype(F32)
    logc = jnp.log(jnp.asarray(np.maximum(count, 1)).astype(bias_table.dtype)).astype(F32)
    return jnp.where(jnp.asarray(count > 0), bias + logc, NEG_INF)


def _sample_attn(q, kn, vn, kc, vc, bias_table):
    b, tq, _ = q.shape
    lw = kc.shape[1]
    nbuf = lw + LANES
    new = pl.BlockSpec((1, tq, A_W), lambda bi: (bi, 0, 0))
    cache = pl.BlockSpec((1, lw, A_W), lambda bi: (bi, 0, 0))
    return pl.pallas_call(
        functools.partial(_sample_attn_kernel, lw=lw, tq=tq),
        grid=(b,),
        in_specs=[new, new, new, cache, cache, _full((H_ATTN, tq, nbuf))],
        out_specs=new,
        out_shape=jax.ShapeDtypeStruct((b, tq, A_W), F32),
        scratch_shapes=[pltpu.VMEM((nbuf, A_W), BF16), pltpu.VMEM((nbuf, A_W), BF16)],
        compiler_params=_params(("parallel",)),
        name="sample_attn",
    )(q, kn, vn, kc, vc, _sample_bias(bias_table, lw, tq, nbuf))


def _softplus(x):
    return jnp.maximum(x, 0.0) + jnp.log(1.0 + jnp.exp(-jnp.abs(x)))


def _rwkv_chunks(states, r, k, v, kk, a, cum, logw, last, c):
    tt = r.shape[0]
    nc = tt // c
    b = kk * a
    w_inv = jnp.exp(-cum)
    w_rem = jnp.exp(last - cum)
    rt = r * jnp.exp(cum)
    at = -kk * jnp.exp(cum - logw)
    bt = b * w_inv
    kt = k * w_inv
    bw = b * w_rem
    kw = k * w_rem
    w_end = jnp.exp(last)

    m0, m1 = _head_masks()
    lane2 = lax.broadcasted_iota(jnp.int32, (1, 2 * PAIR), 1)
    m0w = jnp.where(lane2 >= PAIR, lane2 - PAIR, lane2) < HEAD_DIM
    lane_c = lax.broadcasted_iota(jnp.int32, (c, 2 * c), 1)
    row_c = lax.broadcasted_iota(jnp.int32, (c, 2 * c), 0)
    left = lane_c < c
    col_c = jnp.where(left, lane_c, lane_c - c)
    strict = col_c < row_c
    incl = col_c <= row_c
    bd_c = ((lax.broadcasted_iota(jnp.int32, (2 * c, 2 * c), 0) >= c)
            == (lax.broadcasted_iota(jnp.int32, (2 * c, 2 * c), 1) >= c))
    bd_f = ((lax.broadcasted_iota(jnp.int32, (PAIR, PAIR), 0) >= HEAD_DIM)
            == (lax.broadcasted_iota(jnp.int32, (PAIR, PAIR), 1) >= HEAD_DIM))
    zeros = jnp.zeros((c, PAIR), BF16)

    items = [(n, pr) for n in range(nc) for pr in range(N_PAIRS)]
    cut = lambda x, it: x[it[0] * c:(it[0] + 1) * c, it[1] * PAIR:(it[1] + 1) * PAIR]
    sel = lambda m, x: jnp.where(m, x, jnp.zeros_like(x))
    cat0 = lambda *xs: jnp.concatenate(xs, axis=0)
    cat1 = lambda *xs: jnp.concatenate(xs, axis=1)

    def block_diag(x):
        xb = x.astype(BF16)
        return sel(bd_c, cat0(xb, xb))

    at_i = [cut(at, it).astype(BF16) for it in items]
    rt_i = [cut(rt, it) for it in items]
    v_i = [cut(v, it).astype(BF16) for it in items]
    bk = [cat0(cut(bt, it), cut(kt, it)).astype(BF16) for it in items]
    kb = [cat0(cut(kt, it), cut(bt, it)).astype(BF16) for it in items]
    rtb = [x.astype(BF16) for x in rt_i]
    aa0 = [_dot_nt(cat0(sel(m0, a_), sel(m0, r_)), y_) for a_, r_, y_ in zip(at_i, rtb, bk)]
    aa1 = [_dot_nt(cat0(sel(m1, a_), sel(m1, r_)), y_) for a_, r_, y_ in zip(at_i, rtb, kb)]
    x = [jnp.where(strict, jnp.where(left, p0[:c], p1[:c]), 0.0) for p0, p1 in zip(aa0, aa1)]
    akp = [jnp.where(strict, jnp.where(left, p1[:c], p0[:c]), 0.0) for p0, p1 in zip(aa0, aa1)]
    arb = [jnp.where(incl, jnp.where(left, p0[c:], p1[c:]), 0.0) for p0, p1 in zip(aa0, aa1)]
    ark = [jnp.where(incl, jnp.where(left, p1[c:], p0[c:]), 0.0) for p0, p1 in zip(aa0, aa1)]

    tp = x
    n = c
    while n > 2:
        xd = [block_diag(xi) for xi in x]
        x = [_dot(xi, d) for xi, d in zip(x, xd)]
        xd = [block_diag(xi) for xi in x]
        tp = [t + xi + _dot(t, d) for t, xi, d in zip(tp, x, xd)]
        n //= 2

    akv = [_dot(m_, cat0(sel(m1, v_), sel(m0, v_))) for m_, v_ in zip(akp, v_i)]
    rhs2 = [cat1(cut(at, it), q_) for it, q_ in zip(items, akv)]
    split = lambda x_: cat0(sel(m0w, x_), sel(~m0w, x_))
    pq = [x_ + _dot(t, split(x_.astype(BF16))) for x_, t in zip(rhs2, tp)]
    gy = [_dot(cat1(b_, k_), cat0(split(x_.astype(BF16)), cat1(zeros, sel(m1, v_)), cat1(zeros, sel(m0, v_))))
          for b_, k_, x_, v_ in zip(arb, ark, pq, v_i)]
    pg = [cat0(x_[:, :PAIR], r_ + g_[:, :PAIR]) for x_, r_, g_ in zip(pq, rt_i, gy)]

    states = list(states)
    y_rows = []
    for n in range(nc):
        idx = [n * N_PAIRS + pr for pr in range(N_PAIRS)]
        uy = [_dot_nt(pg[i], states[pr]) for pr, i in enumerate(idx)]
        u = [uy_[:c] + pq[i][:, PAIR:] for uy_, i in zip(uy, idx)]
        y_rows.append(cat1(*[uy_[c:] + gy[i][:, PAIR:] for uy_, i in zip(uy, idx)]))
        upd = [_dot_tn(cat0(u_.astype(BF16), v_i[i]), cat0(cut(bw, items[i]), cut(kw, items[i])))
               for u_, i in zip(u, idx)]
        states = [s * cut(w_end, items[i])[:1] + jnp.where(bd_f, d, 0.0) for s, d, i in zip(states, upd, idx)]
    return cat0(*y_rows), states


def _rwkv_kernel(p_ref, shift0_ref, s0_ref, mu_ref, w0_ref, wl_ref, a0_ref, al_ref, gl_ref,
                 kk_ref, ka_ref, rk_ref, lng_ref, lnb_ref, bd1_ref, bdm_ref,
                 y_ref, s_ref, carry, *, c):
    ci = pl.program_id(1)
    tt = p_ref.shape[1]

    @pl.when(ci == 0)
    def _():
        carry[...] = shift0_ref[0]
        s_ref[0] = s0_ref[0]

    p = p_ref[0]
    row = lax.broadcasted_iota(jnp.int32, (tt, 1), 0)
    prev = jnp.where(row == 0, carry[...], pltpu.roll(p, 1, 0))
    carry[...] = p[tt - 1:tt, :]
    xs = p + (prev - p) * mu_ref[...]
    r = xs[:, :B_W]
    k = xs[:, B_W:2 * B_W]
    v = xs[:, 2 * B_W:3 * B_W]
    xwa = xs[:, 3 * B_W:3 * B_W + LANES]
    xg = xs[:, 3 * B_W + LANES:]
    w_log = -_softplus(-(w0_ref[...] + _dot(jnp.tanh(xwa), wl_ref[...]))) - 0.5
    logw = -jnp.exp(w_log)
    a = jax.nn.sigmoid(a0_ref[...] + _dot(xwa, al_ref[...]))
    g = _dot(jax.nn.sigmoid(xg), gl_ref[...])
    bd1 = bd1_ref[...]
    kk = k * kk_ref[...]
    kk = kk / jnp.maximum(jnp.sqrt(_group_reduce(kk * kk, bd1)), 1e-12)
    k = k * (1.0 + (a - 1.0) * ka_ref[...])

    row_t = lax.broadcasted_iota(jnp.int32, (tt, tt), 0)
    col_t = lax.broadcasted_iota(jnp.int32, (tt, tt), 1)
    chunk_start = (row_t // c) * c
    tri = jnp.where((col_t <= row_t) & (col_t >= chunk_start), 1.0, 0.0).astype(BF16)
    hi, mid, lo = _split3(logw)
    cum = (jnp.dot(tri, hi, preferred_element_type=F32) + jnp.dot(tri, mid, preferred_element_type=F32)
           + jnp.dot(tri, lo, preferred_element_type=F32))
    cum3 = cum.reshape(tt // c, c, B_W)
    last = jnp.broadcast_to(cum3[:, c - 1:c, :], cum3.shape).reshape(tt, B_W)

    y, states = _rwkv_chunks([s_ref[0, pr] for pr in range(N_PAIRS)], r, k, v, kk, a, cum, logw, last, c)
    for pr in range(N_PAIRS):
        s_ref[0, pr] = states[pr]

    bdm = bdm_ref[...]
    dev = y - _group_reduce(y, bdm)
    yn = dev * lax.rsqrt(_group_reduce(dev * dev, bdm) + GN_EPS) * lng_ref[...] + lnb_ref[...]
    bonus = _group_reduce(r * k * rk_ref[...], bd1) * v
    y_ref[0] = ((yn + bonus) * g).astype(BF16)


def _rwkv(p, shift0, s0, weights, chunk, tile):
    b, t, _ = p.shape
    wspecs = [_full(w.shape) for w in weights]
    return pl.pallas_call(
        functools.partial(_rwkv_kernel, c=chunk),
        grid=(b, t // tile),
        in_specs=[pl.BlockSpec((1, tile, P_PAD), lambda bi, ci: (bi, ci, 0)),
                  pl.BlockSpec((1, 1, P_PAD), lambda bi, ci: (bi, 0, 0)),
                  pl.BlockSpec((1, N_PAIRS, PAIR, PAIR), lambda bi, ci: (bi, 0, 0, 0))] + wspecs,
        out_specs=[pl.BlockSpec((1, tile, B_W), lambda bi, ci: (bi, ci, 0)),
                   pl.BlockSpec((1, N_PAIRS, PAIR, PAIR), lambda bi, ci: (bi, 0, 0, 0))],
        out_shape=[jax.ShapeDtypeStruct((b, t, B_W), BF16),
                   jax.ShapeDtypeStruct((b, N_PAIRS, PAIR, PAIR), F32)],
        scratch_shapes=[pltpu.VMEM((1, P_PAD), F32)],
        compiler_params=_params(("parallel", "arbitrary")),
        name="rwkv7",
    )(p, shift0, s0, *weights)


def _pair_states(s):
    b = s.shape[0]
    s = s.reshape(b, N_PAIRS, 2, HEAD_DIM, HEAD_DIM)
    z = jnp.zeros_like(s[:, :, 0])
    top = jnp.concatenate([s[:, :, 0], z], axis=-1)
    bot = jnp.concatenate([z, s[:, :, 1]], axis=-1)
    return jnp.concatenate([top, bot], axis=-2)


def _unpair_states(s):
    return jnp.stack([s[:, :, :HEAD_DIM, :HEAD_DIM], s[:, :, HEAD_DIM:, HEAD_DIM:]], axis=2).reshape(
        s.shape[0], H_RWKV, HEAD_DIM, HEAD_DIM)


def _outmlp_kernel(*refs, n_branch):
    x_ref = refs[0]
    o_refs = refs[1:1 + n_branch]
    lse_refs = refs[1 + n_branch:1 + 2 * n_branch] if n_branch > 1 else ()
    rw_ref, wo_ref, g2_ref, w1_ref, w2_ref, y_ref = refs[1 + n_branch + len(lse_refs):]
    if n_branch > 1:
        lses = [r[...] for r in lse_refs]
        top = functools.reduce(jnp.maximum, lses)
        num = den = None
        for o_ref, lse in zip(o_refs, lses):
            e = jnp.exp(lse - top)
            t = e * o_ref[...].astype(F32)
            num = t if num is None else num + t
            den = e if den is None else den + e
        att = num / den
    else:
        att = o_refs[0][...]
    h = x_ref[...] + _dot(att, wo_ref[:A_W, :]) + jnp.dot(rw_ref[...], wo_ref[A_W:, :],
                                                           preferred_element_type=F32)
    ms = jnp.mean(h * h, axis=-1, keepdims=True)
    m = (h * lax.rsqrt(ms + RMS_EPS) * g2_ref[...]).astype(BF16)
    acc = None
    for c in range(D_FF // D_MODEL):
        sl = slice(c * D_MODEL, (c + 1) * D_MODEL)
        u = jnp.maximum(jnp.dot(m, w1_ref[:, sl], preferred_element_type=F32), 0.0)
        t = jnp.dot((u * u).astype(BF16), w2_ref[sl, :], preferred_element_type=F32)
        acc = t if acc is None else acc + t
    y_ref[...] = h + acc


def _outmlp(x, os_, lses, rw, wo, g2, w1, w2, tm):
    n_tok = x.shape[0]
    tm = min(tm, n_tok)
    n_branch = len(os_)
    row = lambda w: pl.BlockSpec((tm, w), lambda i: (i, 0))
    const = lambda shape: pl.BlockSpec(shape, lambda i: (0, 0), pipeline_mode=pl.Buffered(1))
    return pl.pallas_call(
        functools.partial(_outmlp_kernel, n_branch=n_branch),
        grid=(n_tok // tm,),
        in_specs=[row(D_MODEL)] + [row(A_W)] * (n_branch + len(lses)) + [row(B_W)]
        + [const((D_MODEL, D_MODEL)), const((1, D_MODEL)), const((D_MODEL, D_FF)), const((D_FF, D_MODEL))],
        out_specs=row(D_MODEL),
        out_shape=jax.ShapeDtypeStruct((n_tok, D_MODEL), F32),
        compiler_params=_params(("parallel",)),
        name="outmlp",
    )(x, *os_, *lses, rw, wo, g2, w1, w2)


def _row(v, width=None):
    v = v.reshape(1, -1).astype(F32)
    if width is not None and v.shape[1] < width:
        v = jnp.pad(v, ((0, 0), (0, width - v.shape[1])))
    return v


def _tile_heads(g, n_heads):
    return jnp.tile(g.reshape(1, HEAD_DIM).astype(F32), (1, n_heads))


def kernel(x_prompt, x_sample, cache_k_win, cache_v_win, state_wkv, state_shift, bias_table, ln1_g, w_in,
           q_norm_g, k_norm_g, mu_shift, w0, w_lora2, a0, a_lora2, g_lora2, k_k, k_a, r_k, lnx_g, lnx_b,
           w_out, ln2_g, w_mlp1, w_mlp2):
    depth = w_in.shape[0]
    assert depth == 1, "a deeper stack would repeat the per-layer calls below"
    li = 0
    bp, tp, _ = x_prompt.shape
    bs, ts, _ = x_sample.shape
    keep = min(MAX_WINDOW, tp)

    wqkv = w_in[li][:, :3 * A_W].astype(BF16)
    wp = jnp.pad(w_in[li][:, 3 * A_W:], ((0, 0), (0, P_PAD - N_SHIFT))).astype(BF16)
    gq = _tile_heads(q_norm_g[li], H_ATTN)
    gk = _tile_heads(k_norm_g[li], H_ATTN)
    zrow = lambda n: jnp.zeros((n, B_W), F32)
    rwkv_weights = [
        _row(mu_shift[li], P_PAD), _row(w0[li]),
        jnp.concatenate([w_lora2[li].astype(F32), zrow(LANES - LORA_W)], axis=0).astype(BF16),
        _row(a0[li]),
        jnp.concatenate([zrow(LORA_W), a_lora2[li].astype(F32)], axis=0).astype(BF16),
        jnp.concatenate([g_lora2[li].astype(F32), zrow(LORA_PAD - LORA_G)], axis=0).astype(BF16),
        _row(k_k[li]), _row(k_a[li]), _row(r_k[li]), _row(lnx_g[li]), _row(lnx_b[li]),
        _block_diag(B_W, 1.0), _block_diag(B_W, 1.0 / HEAD_DIM),
    ]
    wo = w_out[li].astype(BF16)
    w1 = w_mlp1[li].astype(BF16)
    w2 = w_mlp2[li].astype(BF16)
    g1 = _row(ln1_g[li])
    g2 = _row(ln2_g[li])

    xp = x_prompt.reshape(bp * tp, D_MODEL)
    q, k, v, kf, vf, p = _inproj(xp, g1, wqkv, wp, gq, gk, tm=512)
    q3, k3, v3 = (a.reshape(bp, tp, A_W) for a in (q, k, v))
    os_, lses = [], []
    for w, d in BRANCHES:
        o, lse = _attn_branch(q3, k3, v3, _branch_bias(bias_table, d), d, tq=1024)
        os_.append(o)
        lses.append(lse)
    p3 = p.reshape(bp, tp, P_PAD)
    rw, s_p = _rwkv(p3, jnp.zeros((bp, 1, P_PAD), F32), jnp.zeros((bp, N_PAIRS, PAIR, PAIR), F32),
                    rwkv_weights, chunk=64, tile=256)
    y_p = _outmlp(xp, os_, lses, rw.reshape(bp * tp, B_W), wo, g2, w1, w2, tm=256)
    heads = lambda a, b_, t_: a.reshape(b_, t_, H_ATTN, HEAD_DIM)
    k_win_p = heads(kf, bp, tp)[:, tp - keep:][None]
    v_win_p = heads(vf, bp, tp)[:, tp - keep:][None]
    wkv_p = _unpair_states(s_p)[None]
    shift_p = p3[:, -1, :N_SHIFT][None]

    xs = x_sample.reshape(bs * ts, D_MODEL)
    qs, _, _, kfs, vfs, ps = _inproj(xs, g1, wqkv, wp, gq, gk, tm=512)
    lw = cache_k_win.shape[2]
    att_s = _sample_attn(qs.reshape(bs, ts, A_W).astype(F32), kfs.reshape(bs, ts, A_W), vfs.reshape(bs, ts, A_W),
                         cache_k_win[li].reshape(bs, lw, A_W), cache_v_win[li].reshape(bs, lw, A_W), bias_table)
    ps3 = ps.reshape(bs, ts, P_PAD)
    shift0 = jnp.pad(state_shift[li].astype(F32), ((0, 0), (0, P_PAD - N_SHIFT))).reshape(bs, 1, P_PAD)
    rw_s, s_s = _rwkv(ps3, shift0, _pair_states(state_wkv[li].astype(F32)), rwkv_weights, chunk=ts, tile=ts)
    y_s = _outmlp(xs, [att_s.reshape(bs * ts, A_W)], [], rw_s.reshape(bs * ts, B_W), wo, g2, w1, w2, tm=256)

    return (y_p.reshape(bp, tp, D_MODEL), y_s.reshape(bs, ts, D_MODEL), k_win_p, v_win_p, wkv_p, shift_p,
            heads(kfs, bs, ts)[None], heads(vfs, bs, ts)[None], _unpair_states(s_s)[None],
            ps3[:, -1, :N_SHIFT][None])
```

```python
import functools
import math

import numpy as np
import jax
import jax.numpy as jnp
from jax import lax
from jax.experimental import pallas as pl
from jax.experimental.pallas import tpu as pltpu

F32 = jnp.float32
BF16 = jnp.bfloat16

D_MODEL = 1024
HEAD_DIM = 64
A_W = 512
B_W = 512
H_ATTN = A_W // HEAD_DIM
H_RWKV = B_W // HEAD_DIM
BRANCHES = ((128, 1), (512, 4), (2048, 16))
MAX_WINDOW = 2048
WIN = 128
NUM_BUCKETS = 32
MAX_DISTANCE = 2048
LORA_W = 64
LORA_A = 64
LORA_G = 160
N_SHIFT = 3 * B_W + LORA_W + LORA_A + LORA_G
D_FF = 4 * D_MODEL
RMS_EPS = 1e-6
GN_EPS = 64e-5
NEG_INF = -1e30

LANES = 128
PAIR = 2 * HEAD_DIM
N_PAIRS = B_W // PAIR
P_PAD = 1920
LORA_PAD = P_PAD - 3 * B_W - LANES
VMEM_LIMIT = 56 * 1024 * 1024


def _dot(a, b):
    return jnp.dot(a.astype(BF16), b.astype(BF16), preferred_element_type=F32)


def _dot_nt(a, b):
    return lax.dot_general(a.astype(BF16), b.astype(BF16), (((1,), (1,)), ((), ())),
                           preferred_element_type=F32)


def _dot_tn(a, b):
    return lax.dot_general(a.astype(BF16), b.astype(BF16), (((0,), (0,)), ((), ())),
                           preferred_element_type=F32)


def _split2(x):
    hi = x.astype(BF16)
    lo = (x - hi.astype(F32)).astype(BF16)
    return hi, lo


def _split3(x):
    hi = x.astype(BF16)
    r1 = x - hi.astype(F32)
    mid = r1.astype(BF16)
    lo = (r1 - mid.astype(F32)).astype(BF16)
    return hi, mid, lo


def _group_reduce(x, bd):
    hi, mid, lo = _split3(x)
    return (jnp.dot(hi, bd, preferred_element_type=F32) + jnp.dot(mid, bd, preferred_element_type=F32)
            + jnp.dot(lo, bd, preferred_element_type=F32))


def _block_diag(n, value):
    idx = np.arange(n) // HEAD_DIM
    return jnp.asarray(np.where(idx[:, None] == idx[None, :], value, 0.0), BF16)


def _full(shape):
    return pl.BlockSpec(shape, lambda *_: (0,) * len(shape))


def _params(sem):
    return pltpu.CompilerParams(dimension_semantics=sem, vmem_limit_bytes=VMEM_LIMIT)


def _inproj_kernel(x_ref, g1_ref, wqkv_ref, wp_ref, gq_ref, gk_ref, bd_ref, *refs, dilations):
    n_view = 3 * len(dilations)
    q_ref, k_ref, v_ref, kf_ref, vf_ref, p_ref = refs[:6]
    view_refs = refs[6:6 + n_view]
    stage = refs[6 + n_view:]
    x = x_ref[...]
    ms = jnp.mean(x * x, axis=-1, keepdims=True)
    n = (x * lax.rsqrt(ms + RMS_EPS) * g1_ref[...]).astype(BF16)
    p_ref[...] = jnp.dot(n, wp_ref[...], preferred_element_type=F32)
    qkv = jnp.dot(n, wqkv_ref[...], preferred_element_type=F32)
    q = qkv[:, :A_W]
    k = qkv[:, A_W:2 * A_W]
    v = qkv[:, 2 * A_W:]
    bd = bd_ref[...]
    qn = q * lax.rsqrt(_group_reduce(q * q, bd) + RMS_EPS) * gq_ref[...]
    kn = k * lax.rsqrt(_group_reduce(k * k, bd) + RMS_EPS) * gk_ref[...]
    qs = qn * (HEAD_DIM ** -0.5)
    q_ref[...] = qs.astype(BF16)
    k_ref[...] = kn.astype(BF16)
    v_ref[...] = v.astype(BF16)
    kf_ref[...] = kn
    vf_ref[...] = v
    if dilations:
        tm = x.shape[0]
        for a, st in zip((qs, kn, v), stage):
            for j in range(A_W // LANES):
                st[j] = a[:, j * LANES:(j + 1) * LANES]
        for di, d in enumerate(dilations):
            for ai, st in enumerate(stage):
                out = view_refs[3 * di + ai]
                for r in range(d):
                    for j in range(A_W // LANES):
                        lo = r * A_W + j * LANES
                        out[:, lo:lo + LANES] = st[j, pl.ds(r, tm // d, stride=d), :].astype(BF16)


def _inproj(x, g1, wqkv, wp, gq, gk, tm, dilations=()):
    n_tok = x.shape[0]
    tm = min(tm, n_tok)
    row = lambda w: pl.BlockSpec((tm, w), lambda i: (i, 0))
    view_specs = [pl.BlockSpec((tm // d, d * A_W), lambda i: (i, 0)) for d in dilations for _ in range(3)]
    view_shapes = [jax.ShapeDtypeStruct((n_tok // d, d * A_W), BF16) for d in dilations for _ in range(3)]
    return pl.pallas_call(
        functools.partial(_inproj_kernel, dilations=dilations),
        grid=(n_tok // tm,),
        in_specs=[row(D_MODEL), _full((1, D_MODEL)), _full((D_MODEL, 3 * A_W)), _full((D_MODEL, P_PAD)),
                  _full((1, A_W)), _full((1, A_W)), _full((A_W, A_W))],
        out_specs=[row(A_W), row(A_W), row(A_W), row(A_W), row(A_W), row(P_PAD)] + view_specs,
        out_shape=[jax.ShapeDtypeStruct((n_tok, A_W), BF16)] * 3
        + [jax.ShapeDtypeStruct((n_tok, A_W), F32)] * 2
        + [jax.ShapeDtypeStruct((n_tok, P_PAD), F32)] + view_shapes,
        scratch_shapes=[pltpu.VMEM((A_W // LANES, tm, LANES), F32)] * (3 if dilations else 0),
        compiler_params=_params(("parallel",)),
        name="inproj",
    )(x, g1, wqkv, wp, gq, gk, _block_diag(A_W, 1.0 / HEAD_DIM))


def _head_masks():
    lane = lax.broadcasted_iota(jnp.int32, (1, PAIR), 1)
    return lane < HEAD_DIM, lane >= HEAD_DIM


def _attn_kernel(q_ref, kh_ref, kc_ref, vh_ref, vc_ref, bias_ref, o_ref, lse_ref, kbuf, vbuf, *, tq):
    i = pl.program_id(2)
    kbuf[0:WIN, :] = kh_ref[0]
    kbuf[WIN:, :] = kc_ref[0]
    vbuf[0:WIN, :] = vh_ref[0]
    vbuf[WIN:, :] = vc_ref[0]
    masks = _head_masks()

    def body(j, carry):
        off = pl.multiple_of(j * WIN, WIN)
        first = jnp.logical_and(i == 0, j == 0).astype(jnp.int32)
        for pr in range(N_PAIRS):
            sl = slice(pr * PAIR, (pr + 1) * PAIR)
            q2 = q_ref[0, pl.ds(off, WIN), sl]
            k2 = kbuf[pl.ds(off, 2 * WIN), sl]
            v2 = vbuf[pl.ds(off, 2 * WIN), sl]
            o_pair = None
            lse_pair = None
            for hh in range(2):
                qm = jnp.where(masks[hh], q2, jnp.zeros_like(q2))
                s = _dot_nt(qm, k2) + bias_ref[first, 2 * pr + hh]
                m = jnp.max(s, axis=-1, keepdims=True)
                e = jnp.exp(s - m)
                l = jnp.sum(e, axis=-1, keepdims=True)
                o = jnp.dot(e.astype(BF16), v2, preferred_element_type=F32) / l
                lse = jnp.broadcast_to(m + jnp.log(l), o.shape)
                o_pair = o if hh == 0 else jnp.where(masks[0], o_pair, o)
                lse_pair = lse if hh == 0 else jnp.where(masks[0], lse_pair, lse)
            o_ref[0, pl.ds(off, WIN), sl] = o_pair.astype(BF16)
            lse_ref[0, pl.ds(off, WIN), sl] = lse_pair
        return carry

    lax.fori_loop(0, tq // WIN, body, 0)


def _attn_branch(q, k, v, bias, d, tq):
    b, l, _ = q.shape
    tq = min(tq, l)
    sub = tq // WIN
    cur = pl.BlockSpec((1, tq, A_W), lambda bi, r, i: (bi, i, r))
    halo = pl.BlockSpec((1, WIN, A_W), lambda bi, r, i: (bi, jnp.maximum(i * sub - 1, 0), r))
    return pl.pallas_call(
        functools.partial(_attn_kernel, tq=tq),
        grid=(b, d, l // tq),
        in_specs=[cur, halo, cur, halo, cur, _full((2, H_ATTN, WIN, 2 * WIN))],
        out_specs=[cur, cur],
        out_shape=[jax.ShapeDtypeStruct((b, l, d * A_W), BF16), jax.ShapeDtypeStruct((b, l, d * A_W), F32)],
        scratch_shapes=[pltpu.VMEM((tq + WIN, A_W), BF16), pltpu.VMEM((tq + WIN, A_W), BF16)],
        compiler_params=_params(("parallel", "parallel", "arbitrary")),
        name=f"attn_d{d}",
    )(q, k, k, v, v, bias)


def _t5_bucket(dist):
    dist = np.maximum(dist, 0)
    max_exact = NUM_BUCKETS // 2
    large = max_exact + (np.log(np.maximum(dist, 1) / max_exact)
                         / math.log(MAX_DISTANCE / max_exact)
                         * (NUM_BUCKETS - max_exact)).astype(np.int32)
    large = np.minimum(large, NUM_BUCKETS - 1)
    return np.where(dist < max_exact, dist, large).astype(np.int32)


def _branch_bias(bias_table, d):
    m = np.arange(WIN)[:, None]
    n = np.arange(2 * WIN)[None, :]
    steps = m + WIN - n
    valid = (steps >= 0) & (steps <= WIN)
    bias = jnp.transpose(bias_table[_t5_bucket(steps * d)], (2, 0, 1)).astype(F32)
    normal = jnp.where(jnp.asarray(valid), bias, NEG_INF)
    first = jnp.where(jnp.asarray(valid & (n >= WIN)), bias, NEG_INF)
    return jnp.stack([normal, first])


def _sample_attn_kernel(q_ref, kn_ref, vn_ref, kc_ref, vc_ref, bias_ref, o_ref, kbuf, vbuf, *, lw, tq):
    kbuf[0:lw, :] = kc_ref[0].astype(BF16)
    vbuf[0:lw, :] = vc_ref[0].astype(BF16)
    pad = jnp.zeros((kbuf.shape[0] - lw, A_W), BF16)
    kbuf[lw:, :] = pad
    vbuf[lw:, :] = pad
    kbuf[lw:lw + tq, :] = kn_ref[0].astype(BF16)
    vbuf[lw:lw + tq, :] = vn_ref[0].astype(BF16)
    masks = _head_masks()
    q = q_ref[0].astype(BF16)
    for pr in range(N_PAIRS):
        sl = slice(pr * PAIR, (pr + 1) * PAIR)
        q2 = q[:, sl]
        k2 = kbuf[:, sl]
        v2 = vbuf[:, sl]
        o_pair = None
        for hh in range(2):
            qm = jnp.where(masks[hh], q2, jnp.zeros_like(q2))
            s = _dot_nt(qm, k2) + bias_ref[2 * pr + hh]
            m = jnp.max(s, axis=-1, keepdims=True)
            e = jnp.exp(s - m)
            l = jnp.sum(e, axis=-1, keepdims=True)
            o = jnp.dot(e.astype(BF16), v2, preferred_element_type=F32) / l
            o_pair = o if hh == 0 else jnp.where(masks[0], o_pair, o)
        o_ref[0, :, sl] = o_pair


def _sample_bias(bias_table, lw, tq, nbuf):
    i = np.arange(tq)[:, None]
    row = np.arange(nbuf)[None, :]
    dist = lw + i - row
    live = (dist >= 0) & (row < lw + tq)
    count = np.zeros(dist.shape, np.int32)
    for w, d in BRANCHES:
        count += (live & (dist <= w) & (dist % d == 0)).astype(np.int32)
    bias = jnp.transpose(bias_table[_t5_bucket(dist)], (2, 0, 1)).astype(F32)
    logc = jnp.log(jnp.asarray(np.maximum(count, 1)).astype(bias_table.dtype)).astype(F32)
    return jnp.where(jnp.asarray(count > 0), bias + logc, NEG_INF)


def _sample_attn(q, kn, vn, kc, vc, bias_table):
    b, tq, _ = q.shape
    lw = kc.shape[1]
    nbuf = lw + LANES
    new = pl.BlockSpec((1, tq, A_W), lambda bi: (bi, 0, 0))
    cache = pl.BlockSpec((1, lw, A_W), lambda bi: (bi, 0, 0))
    return pl.pallas_call(
        functools.partial(_sample_attn_kernel, lw=lw, tq=tq),
        grid=(b,),
        in_specs=[new, new, new, cache, cache, _full((H_ATTN, tq, nbuf))],
        out_specs=new,
        out_shape=jax.ShapeDtypeStruct((b, tq, A_W), F32),
        scratch_shapes=[pltpu.VMEM((nbuf, A_W), BF16), pltpu.VMEM((nbuf, A_W), BF16)],
        compiler_params=_params(("parallel",)),
        name="sample_attn",
    )(q, kn, vn, kc, vc, _sample_bias(bias_table, lw, tq, nbuf))


def _softplus(x):
    return jnp.maximum(x, 0.0) + jnp.log(1.0 + jnp.exp(-jnp.abs(x)))


def _rwkv_chunks(states, r, k, v, kk, a, cum, logw, last, c):
    tt = r.shape[0]
    nc = tt // c
    b = kk * a
    w_inv = jnp.exp(-cum)
    w_rem = jnp.exp(last - cum)
    rt = r * jnp.exp(cum)
    at = -kk * jnp.exp(cum - logw)
    bt = b * w_inv
    kt = k * w_inv
    bw = b * w_rem
    kw = k * w_rem
    w_end = jnp.exp(last)

    m0, m1 = _head_masks()
    lane2 = lax.broadcasted_iota(jnp.int32, (1, 2 * PAIR), 1)
    m0w = jnp.where(lane2 >= PAIR, lane2 - PAIR, lane2) < HEAD_DIM
    lane_c = lax.broadcasted_iota(jnp.int32, (c, 2 * c), 1)
    row_c = lax.broadcasted_iota(jnp.int32, (c, 2 * c), 0)
    left = lane_c < c
    col_c = jnp.where(left, lane_c, lane_c - c)
    strict = col_c < row_c
    incl = col_c <= row_c
    bd_c = ((lax.broadcasted_iota(jnp.int32, (2 * c, 2 * c), 0) >= c)
            == (lax.broadcasted_iota(jnp.int32, (2 * c, 2 * c), 1) >= c))
    bd_f = ((lax.broadcasted_iota(jnp.int32, (PAIR, PAIR), 0) >= HEAD_DIM)
            == (lax.broadcasted_iota(jnp.int32, (PAIR, PAIR), 1) >= HEAD_DIM))
    zeros = jnp.zeros((c, PAIR), BF16)

    items = [(n, pr) for n in range(nc) for pr in range(N_PAIRS)]
    cut = lambda x, it: x[it[0] * c:(it[0] + 1) * c, it[1] * PAIR:(it[1] + 1) * PAIR]
    sel = lambda m, x: jnp.where(m, x, jnp.zeros_like(x))
    cat0 = lambda *xs: jnp.concatenate(xs, axis=0)
    cat1 = lambda *xs: jnp.concatenate(xs, axis=1)

    def block_diag(x):
        xb = x.astype(BF16)
        return sel(bd_c, cat0(xb, xb))

    at_i = [cut(at, it).astype(BF16) for it in items]
    rt_i = [cut(rt, it) for it in items]
    v_i = [cut(v, it).astype(BF16) for it in items]
    bk = [cat0(cut(bt, it), cut(kt, it)).astype(BF16) for it in items]
    kb = [cat0(cut(kt, it), cut(bt, it)).astype(BF16) for it in items]
    rtb = [x.astype(BF16) for x in rt_i]
    aa0 = [_dot_nt(cat0(sel(m0, a_), sel(m0, r_)), y_) for a_, r_, y_ in zip(at_i, rtb, bk)]
    aa1 = [_dot_nt(cat0(sel(m1, a_), sel(m1, r_)), y_) for a_, r_, y_ in zip(at_i, rtb, kb)]
    x = [jnp.where(strict, jnp.where(left, p0[:c], p1[:c]), 0.0) for p0, p1 in zip(aa0, aa1)]
    akp = [jnp.where(strict, jnp.where(left, p1[:c], p0[:c]), 0.0) for p0, p1 in zip(aa0, aa1)]
    arb = [jnp.where(incl, jnp.where(left, p0[c:], p1[c:]), 0.0) for p0, p1 in zip(aa0, aa1)]
    ark = [jnp.where(incl, jnp.where(left, p1[c:], p0[c:]), 0.0) for p0, p1 in zip(aa0, aa1)]

    tp = x
    n = c
    while n > 2:
        xd = [block_diag(xi) for xi in x]
        x = [_dot(xi, d) for xi, d in zip(x, xd)]
        xd = [block_diag(xi) for xi in x]
        tp = [t + xi + _dot(t, d) for t, xi, d in zip(tp, x, xd)]
        n //= 2

    akv = [_dot(m_, cat0(sel(m1, v_), sel(m0, v_))) for m_, v_ in zip(akp, v_i)]
    rhs2 = [cat1(cut(at, it), q_) for it, q_ in zip(items, akv)]
    split = lambda x_: cat0(sel(m0w, x_), sel(~m0w, x_))
    pq = [x_ + _dot(t, split(x_.astype(BF16))) for x_, t in zip(rhs2, tp)]
    gy = [_dot(cat1(b_, k_), cat0(split(x_.astype(BF16)), cat1(zeros, sel(m1, v_)), cat1(zeros, sel(m0, v_))))
          for b_, k_, x_, v_ in zip(arb, ark, pq, v_i)]
    pg = [cat0(x_[:, :PAIR], r_ + g_[:, :PAIR]) for x_, r_, g_ in zip(pq, rt_i, gy)]

    states = list(states)
    y_rows = []
    for n in range(nc):
        idx = [n * N_PAIRS + pr for pr in range(N_PAIRS)]
        uy = [_dot_nt(pg[i], states[pr]) for pr, i in enumerate(idx)]
        u = [uy_[:c] + pq[i][:, PAIR:] for uy_, i in zip(uy, idx)]
        y_rows.append(cat1(*[uy_[c:] + gy[i][:, PAIR:] for uy_, i in zip(uy, idx)]))
        upd = [_dot_tn(cat0(u_.astype(BF16), v_i[i]), cat0(cut(bw, items[i]), cut(kw, items[i])))
               for u_, i in zip(u, idx)]
        states = [s * cut(w_end, items[i])[:1] + jnp.where(bd_f, d, 0.0) for s, d, i in zip(states, upd, idx)]
    return cat0(*y_rows), states


def _rwkv_kernel(p_ref, shift0_ref, s0_ref, mu_ref, w0_ref, wl_ref, a0_ref, al_ref, gl_ref,
                 kk_ref, ka_ref, rk_ref, lng_ref, lnb_ref, bd1_ref, bdm_ref,
                 y_ref, s_ref, carry, *, c):
    ci = pl.program_id(1)
    tt = p_ref.shape[1]

    @pl.when(ci == 0)
    def _():
        carry[...] = shift0_ref[0]
        s_ref[0] = s0_ref[0]

    p = p_ref[0]
    row = lax.broadcasted_iota(jnp.int32, (tt, 1), 0)
    prev = jnp.where(row == 0, carry[...], pltpu.roll(p, 1, 0))
    carry[...] = p[tt - 1:tt, :]
    xs = p + (prev - p) * mu_ref[...]
    r = xs[:, :B_W]
    k = xs[:, B_W:2 * B_W]
    v = xs[:, 2 * B_W:3 * B_W]
    xwa = xs[:, 3 * B_W:3 * B_W + LANES]
    xg = xs[:, 3 * B_W + LANES:]
    w_log = -_softplus(-(w0_ref[...] + _dot(jnp.tanh(xwa), wl_ref[...]))) - 0.5
    logw = -jnp.exp(w_log)
    a = jax.nn.sigmoid(a0_ref[...] + _dot(xwa, al_ref[...]))
    g = _dot(jax.nn.sigmoid(xg), gl_ref[...])
    bd1 = bd1_ref[...]
    kk = k * kk_ref[...]
    kk = kk / jnp.maximum(jnp.sqrt(_group_reduce(kk * kk, bd1)), 1e-12)
    k = k * (1.0 + (a - 1.0) * ka_ref[...])

    row_t = lax.broadcasted_iota(jnp.int32, (tt, tt), 0)
    col_t = lax.broadcasted_iota(jnp.int32, (tt, tt), 1)
    chunk_start = (row_t // c) * c
    tri = jnp.where((col_t <= row_t) & (col_t >= chunk_start), 1.0, 0.0).astype(BF16)
    hi, mid, lo = _split3(logw)
    cum = (jnp.dot(tri, hi, preferred_element_type=F32) + jnp.dot(tri, mid, preferred_element_type=F32)
           + jnp.dot(tri, lo, preferred_element_type=F32))
    cum3 = cum.reshape(tt // c, c, B_W)
    last = jnp.broadcast_to(cum3[:, c - 1:c, :], cum3.shape).reshape(tt, B_W)

    y, states = _rwkv_chunks([s_ref[0, pr] for pr in range(N_PAIRS)], r, k, v, kk, a, cum, logw, last, c)
    for pr in range(N_PAIRS):
        s_ref[0, pr] = states[pr]

    bdm = bdm_ref[...]
    dev = y - _group_reduce(y, bdm)
    yn = dev * lax.rsqrt(_group_reduce(dev * dev, bdm) + GN_EPS) * lng_ref[...] + lnb_ref[...]
    bonus = _group_reduce(r * k * rk_ref[...], bd1) * v
    y_ref[0] = ((yn + bonus) * g).astype(BF16)


def _rwkv(p, shift0, s0, weights, chunk, tile):
    b, t, _ = p.shape
    wspecs = [_full(w.shape) for w in weights]
    return pl.pallas_call(
        functools.partial(_rwkv_kernel, c=chunk),
        grid=(b, t // tile),
        in_specs=[pl.BlockSpec((1, tile, P_PAD), lambda bi, ci: (bi, ci, 0)),
                  pl.BlockSpec((1, 1, P_PAD), lambda bi, ci: (bi, 0, 0)),
                  pl.BlockSpec((1, N_PAIRS, PAIR, PAIR), lambda bi, ci: (bi, 0, 0, 0))] + wspecs,
        out_specs=[pl.BlockSpec((1, tile, B_W), lambda bi, ci: (bi, ci, 0)),
                   pl.BlockSpec((1, N_PAIRS, PAIR, PAIR), lambda bi, ci: (bi, 0, 0, 0))],
        out_shape=[jax.ShapeDtypeStruct((b, t, B_W), BF16),
                   jax.ShapeDtypeStruct((b, N_PAIRS, PAIR, PAIR), F32)],
        scratch_shapes=[pltpu.VMEM((1, P_PAD), F32)],
        compiler_params=_params(("parallel", "arbitrary")),
        name="rwkv7",
    )(p, shift0, s0, *weights)


def _pair_states(s):
    b = s.shape[0]
    s = s.reshape(b, N_PAIRS, 2, HEAD_DIM, HEAD_DIM)
    z = jnp.zeros_like(s[:, :, 0])
    top = jnp.concatenate([s[:, :, 0], z], axis=-1)
    bot = jnp.concatenate([z, s[:, :, 1]], axis=-1)
    return jnp.concatenate([top, bot], axis=-2)


def _unpair_states(s):
    return jnp.stack([s[:, :, :HEAD_DIM, :HEAD_DIM], s[:, :, HEAD_DIM:, HEAD_DIM:]], axis=2).reshape(
        s.shape[0], H_RWKV, HEAD_DIM, HEAD_DIM)


def _outmlp_kernel(*refs, dilations):
    n_branch = len(dilations)
    x_ref = refs[0]
    o_refs = refs[1:1 + n_branch]
    lse_refs = refs[1 + n_branch:1 + 2 * n_branch] if n_branch > 1 else ()
    n_in = 1 + n_branch + len(lse_refs)
    rw_ref, wo_ref, g2_ref, w1_ref, w2_ref, y_ref = refs[n_in:n_in + 6]
    stage = refs[n_in + 6:]
    tm = x_ref.shape[0]

    def natural(ref, d, st):
        if d == 1:
            return ref[...].astype(F32)
        for r in range(d):
            for j in range(A_W // LANES):
                lo = r * A_W + j * LANES
                st[j, pl.ds(r, tm // d, stride=d), :] = ref[:, lo:lo + LANES].astype(F32)
        return jnp.concatenate([st[j] for j in range(A_W // LANES)], axis=1)

    if n_branch > 1:
        lses = [natural(ref, d, stage[2 * i]) for i, (ref, d) in enumerate(zip(lse_refs, dilations))]
        top = functools.reduce(jnp.maximum, lses)
        num = den = None
        for i, (o_ref, d, lse) in enumerate(zip(o_refs, dilations, lses)):
            e = jnp.exp(lse - top)
            t = e * natural(o_ref, d, stage[2 * i + 1])
            num = t if num is None else num + t
            den = e if den is None else den + e
        att = num / den
    else:
        att = o_refs[0][...]
    h = x_ref[...] + _dot(att, wo_ref[:A_W, :]) + jnp.dot(rw_ref[...], wo_ref[A_W:, :],
                                                           preferred_element_type=F32)
    ms = jnp.mean(h * h, axis=-1, keepdims=True)
    m = (h * lax.rsqrt(ms + RMS_EPS) * g2_ref[...]).astype(BF16)
    acc = None
    for c in range(D_FF // D_MODEL):
        sl = slice(c * D_MODEL, (c + 1) * D_MODEL)
        u = jnp.maximum(jnp.dot(m, w1_ref[:, sl], preferred_element_type=F32), 0.0)
        t = jnp.dot((u * u).astype(BF16), w2_ref[sl, :], preferred_element_type=F32)
        acc = t if acc is None else acc + t
    y_ref[...] = h + acc


def _outmlp(x, os_, lses, dilations, rw, wo, g2, w1, w2, tm):
    n_tok = x.shape[0]
    tm = min(tm, n_tok)
    row = lambda w: pl.BlockSpec((tm, w), lambda i: (i, 0))
    view = lambda d: pl.BlockSpec((tm // d, d * A_W), lambda i: (i, 0))
    const = lambda shape: pl.BlockSpec(shape, lambda i: (0, 0), pipeline_mode=pl.Buffered(1))
    n_stage = 2 * len(dilations) if len(dilations) > 1 else 0
    return pl.pallas_call(
        functools.partial(_outmlp_kernel, dilations=dilations),
        grid=(n_tok // tm,),
        in_specs=[row(D_MODEL)] + [view(d) for d in dilations] + [view(d) for d in dilations[:len(lses)]]
        + [row(B_W), const((D_MODEL, D_MODEL)), const((1, D_MODEL)), const((D_MODEL, D_FF)),
           const((D_FF, D_MODEL))],
        out_specs=row(D_MODEL),
        out_shape=jax.ShapeDtypeStruct((n_tok, D_MODEL), F32),
        scratch_shapes=[pltpu.VMEM((A_W // LANES, tm, LANES), F32)] * n_stage,
        compiler_params=_params(("parallel",)),
        name="outmlp",
    )(x, *os_, *lses, rw, wo, g2, w1, w2)


def _row(v, width=None):
    v = v.reshape(1, -1).astype(F32)
    if width is not None and v.shape[1] < width:
        v = jnp.pad(v, ((0, 0), (0, width - v.shape[1])))
    return v


def _tile_heads(g, n_heads):
    return jnp.tile(g.reshape(1, HEAD_DIM).astype(F32), (1, n_heads))


def kernel(x_prompt, x_sample, cache_k_win, cache_v_win, state_wkv, state_shift, bias_table, ln1_g, w_in,
           q_norm_g, k_norm_g, mu_shift, w0, w_lora2, a0, a_lora2, g_lora2, k_k, k_a, r_k, lnx_g, lnx_b,
           w_out, ln2_g, w_mlp1, w_mlp2):
    depth = w_in.shape[0]
    assert depth == 1, "a deeper stack would repeat the per-layer calls below"
    li = 0
    bp, tp, _ = x_prompt.shape
    bs, ts, _ = x_sample.shape
    keep = min(MAX_WINDOW, tp)

    wqkv = w_in[li][:, :3 * A_W].astype(BF16)
    wp = jnp.pad(w_in[li][:, 3 * A_W:], ((0, 0), (0, P_PAD - N_SHIFT))).astype(BF16)
    gq = _tile_heads(q_norm_g[li], H_ATTN)
    gk = _tile_heads(k_norm_g[li], H_ATTN)
    zrow = lambda n: jnp.zeros((n, B_W), F32)
    rwkv_weights = [
        _row(mu_shift[li], P_PAD), _row(w0[li]),
        jnp.concatenate([w_lora2[li].astype(F32), zrow(LANES - LORA_W)], axis=0).astype(BF16),
        _row(a0[li]),
        jnp.concatenate([zrow(LORA_W), a_lora2[li].astype(F32)], axis=0).astype(BF16),
        jnp.concatenate([g_lora2[li].astype(F32), zrow(LORA_PAD - LORA_G)], axis=0).astype(BF16),
        _row(k_k[li]), _row(k_a[li]), _row(r_k[li]), _row(lnx_g[li]), _row(lnx_b[li]),
        _block_diag(B_W, 1.0), _block_diag(B_W, 1.0 / HEAD_DIM),
    ]
    wo = w_out[li].astype(BF16)
    w1 = w_mlp1[li].astype(BF16)
    w2 = w_mlp2[li].astype(BF16)
    g1 = _row(ln1_g[li])
    g2 = _row(ln2_g[li])

    xp = x_prompt.reshape(bp * tp, D_MODEL)
    dils = tuple(d for _, d in BRANCHES)
    q, k, v, kf, vf, p, *views = _inproj(xp, g1, wqkv, wp, gq, gk, tm=512, dilations=dils[1:])
    views = [q, k, v] + views
    os_, lses = [], []
    for i, d in enumerate(dils):
        qd, kd, vd = (a.reshape(bp, tp // d, d * A_W) for a in views[3 * i:3 * i + 3])
        o, lse = _attn_branch(qd, kd, vd, _branch_bias(bias_table, d), d, tq=1024)
        os_.append(o.reshape(bp * tp // d, d * A_W))
        lses.append(lse.reshape(bp * tp // d, d * A_W))
    p3 = p.reshape(bp, tp, P_PAD)
    rw, s_p = _rwkv(p3, jnp.zeros((bp, 1, P_PAD), F32), jnp.zeros((bp, N_PAIRS, PAIR, PAIR), F32),
                    rwkv_weights, chunk=64, tile=256)
    y_p = _outmlp(xp, os_, lses, dils, rw.reshape(bp * tp, B_W), wo, g2, w1, w2, tm=256)
    heads = lambda a, b_, t_: a.reshape(b_, t_, H_ATTN, HEAD_DIM)
    k_win_p = heads(kf, bp, tp)[:, tp - keep:][None]
    v_win_p = heads(vf, bp, tp)[:, tp - keep:][None]
    wkv_p = _unpair_states(s_p)[None]
    shift_p = p3[:, -1, :N_SHIFT][None]

    xs = x_sample.reshape(bs * ts, D_MODEL)
    qs, _, _, kfs, vfs, ps = _inproj(xs, g1, wqkv, wp, gq, gk, tm=512)
    lw = cache_k_win.shape[2]
    att_s = _sample_attn(qs.reshape(bs, ts, A_W).astype(F32), kfs.reshape(bs, ts, A_W), vfs.reshape(bs, ts, A_W),
                         cache_k_win[li].reshape(bs, lw, A_W), cache_v_win[li].reshape(bs, lw, A_W), bias_table)
    ps3 = ps.reshape(bs, ts, P_PAD)
    shift0 = jnp.pad(state_shift[li].astype(F32), ((0, 0), (0, P_PAD - N_SHIFT))).reshape(bs, 1, P_PAD)
    rw_s, s_s = _rwkv(ps3, shift0, _pair_states(state_wkv[li].astype(F32)), rwkv_weights, chunk=ts, tile=ts)
    y_s = _outmlp(xs, [att_s.reshape(bs * ts, A_W)], [], (1,), rw_s.reshape(bs * ts, B_W), wo, g2, w1, w2, tm=256)

    return (y_p.reshape(bp, tp, D_MODEL), y_s.reshape(bs, ts, D_MODEL), k_win_p, v_win_p, wkv_p, shift_p,
            heads(kfs, bs, ts)[None], heads(vfs, bs, ts)[None], _unpair_states(s_s)[None],
            ps3[:, -1, :N_SHIFT][None])
```

```python
import functools
import math

import numpy as np
import jax
import jax.numpy as jnp
from jax import lax
from jax.experimental import pallas as pl
from jax.experimental.pallas import tpu as pltpu

F32 = jnp.float32
BF16 = jnp.bfloat16

D_MODEL = 1024
HEAD_DIM = 64
A_W = 512
B_W = 512
H_ATTN = A_W // HEAD_DIM
H_RWKV = B_W // HEAD_DIM
BRANCHES = ((128, 1), (512, 4), (2048, 16))
MAX_WINDOW = 2048
WIN = 128
NUM_BUCKETS = 32
MAX_DISTANCE = 2048
LORA_W = 64
LORA_A = 64
LORA_G = 160
N_SHIFT = 3 * B_W + LORA_W + LORA_A + LORA_G
D_FF = 4 * D_MODEL
RMS_EPS = 1e-6
GN_EPS = 64e-5
NEG_INF = -1e30

LANES = 128
PAIR = 2 * HEAD_DIM
N_PAIRS = B_W // PAIR
P_PAD = 1920
LORA_PAD = P_PAD - 3 * B_W - LANES
VMEM_LIMIT = 56 * 1024 * 1024
SCORES_AHEAD = 8


def _dot(a, b):
    return jnp.dot(a.astype(BF16), b.astype(BF16), preferred_element_type=F32)


def _dot_nt(a, b):
    return lax.dot_general(a.astype(BF16), b.astype(BF16), (((1,), (1,)), ((), ())),
                           preferred_element_type=F32)


def _dot_tn(a, b):
    return lax.dot_general(a.astype(BF16), b.astype(BF16), (((0,), (0,)), ((), ())),
                           preferred_element_type=F32)


def _split2(x):
    hi = x.astype(BF16)
    lo = (x - hi.astype(F32)).astype(BF16)
    return hi, lo


def _split3(x):
    hi = x.astype(BF16)
    r1 = x - hi.astype(F32)
    mid = r1.astype(BF16)
    lo = (r1 - mid.astype(F32)).astype(BF16)
    return hi, mid, lo


def _group_reduce(x, bd):
    hi, lo = _split2(x)
    return jnp.dot(hi, bd, preferred_element_type=F32) + jnp.dot(lo, bd, preferred_element_type=F32)


def _block_diag(n, value):
    idx = np.arange(n) // HEAD_DIM
    return jnp.asarray(np.where(idx[:, None] == idx[None, :], value, 0.0), BF16)


def _full(shape):
    return pl.BlockSpec(shape, lambda *_: (0,) * len(shape))


def _params(sem):
    return pltpu.CompilerParams(dimension_semantics=sem, vmem_limit_bytes=VMEM_LIMIT)


def _inproj_kernel(x_ref, g1_ref, wqkv_ref, wp_ref, gq_ref, gk_ref, bd_ref, *refs, dilations):
    n_view = 3 * len(dilations)
    q_ref, k_ref, v_ref, kf_ref, vf_ref, p_ref = refs[:6]
    view_refs = refs[6:6 + n_view]
    stage = refs[6 + n_view:]
    x = x_ref[...]
    ms = jnp.mean(x * x, axis=-1, keepdims=True)
    n = (x * lax.rsqrt(ms + RMS_EPS) * g1_ref[...]).astype(BF16)
    p_ref[...] = jnp.dot(n, wp_ref[...], preferred_element_type=F32)
    qkv = jnp.dot(n, wqkv_ref[...], preferred_element_type=F32)
    q = qkv[:, :A_W]
    k = qkv[:, A_W:2 * A_W]
    v = qkv[:, 2 * A_W:]
    bd = bd_ref[...]
    qn = q * lax.rsqrt(_group_reduce(q * q, bd) + RMS_EPS) * gq_ref[...]
    kn = k * lax.rsqrt(_group_reduce(k * k, bd) + RMS_EPS) * gk_ref[...]
    qs = qn * (HEAD_DIM ** -0.5)
    q_ref[...] = qs.astype(BF16)
    k_ref[...] = kn.astype(BF16)
    v_ref[...] = v.astype(BF16)
    kf_ref[...] = kn
    vf_ref[...] = v
    if dilations:
        tm = x.shape[0]
        for a, st in zip((qs, kn, v), stage):
            for j in range(A_W // LANES):
                st[j] = a[:, j * LANES:(j + 1) * LANES]
        for di, d in enumerate(dilations):
            for ai, st in enumerate(stage):
                out = view_refs[3 * di + ai]
                for r in range(d):
                    for j in range(A_W // LANES):
                        lo = r * A_W + j * LANES
                        out[:, lo:lo + LANES] = st[j, pl.ds(r, tm // d, stride=d), :].astype(BF16)


def _inproj(x, g1, wqkv, wp, gq, gk, tm, dilations=()):
    n_tok = x.shape[0]
    tm = min(tm, n_tok)
    row = lambda w: pl.BlockSpec((tm, w), lambda i: (i, 0))
    view_specs = [pl.BlockSpec((tm // d, d * A_W), lambda i: (i, 0)) for d in dilations for _ in range(3)]
    view_shapes = [jax.ShapeDtypeStruct((n_tok // d, d * A_W), BF16) for d in dilations for _ in range(3)]
    return pl.pallas_call(
        functools.partial(_inproj_kernel, dilations=dilations),
        grid=(n_tok // tm,),
        in_specs=[row(D_MODEL), _full((1, D_MODEL)), _full((D_MODEL, 3 * A_W)), _full((D_MODEL, P_PAD)),
                  _full((1, A_W)), _full((1, A_W)), _full((A_W, A_W))],
        out_specs=[row(A_W), row(A_W), row(A_W), row(A_W), row(A_W), row(P_PAD)] + view_specs,
        out_shape=[jax.ShapeDtypeStruct((n_tok, A_W), BF16)] * 3
        + [jax.ShapeDtypeStruct((n_tok, A_W), F32)] * 2
        + [jax.ShapeDtypeStruct((n_tok, P_PAD), F32)] + view_shapes,
        scratch_shapes=[pltpu.VMEM((A_W // LANES, tm, LANES), F32)] * (3 if dilations else 0),
        compiler_params=_params(("parallel",)),
        name="inproj",
    )(x, g1, wqkv, wp, gq, gk, _block_diag(A_W, 1.0 / HEAD_DIM))


def _head_masks():
    lane = lax.broadcasted_iota(jnp.int32, (1, PAIR), 1)
    return lane < HEAD_DIM, lane >= HEAD_DIM


def _attn_kernel(q_ref, kh_ref, kc_ref, vh_ref, vc_ref, bias_ref, o_ref, lse_ref, kbuf, vbuf, *, tq):
    i = pl.program_id(2)
    kbuf[0:WIN, :] = kh_ref[0]
    kbuf[WIN:, :] = kc_ref[0]
    vbuf[0:WIN, :] = vh_ref[0]
    vbuf[WIN:, :] = vc_ref[0]
    masks = _head_masks()

    def body(j, carry):
        off = pl.multiple_of(j * WIN, WIN)
        first = jnp.logical_and(i == 0, j == 0).astype(jnp.int32)
        pair = lambda h: slice((h // 2) * PAIR, (h // 2 + 1) * PAIR)

        def scores(h):
            q2 = q_ref[0, pl.ds(off, WIN), pair(h)]
            qm = jnp.where(masks[h % 2], q2, jnp.zeros_like(q2))
            return _dot_nt(qm, kbuf[pl.ds(off, 2 * WIN), pair(h)]) + bias_ref[first, h]

        ahead = SCORES_AHEAD
        pending = [scores(h) for h in range(ahead)]
        o_pair = lse_pair = None
        for h in range(H_ATTN):
            if h + ahead < H_ATTN:
                pending.append(scores(h + ahead))
            s = pending.pop(0)
            m = jnp.max(s, axis=-1, keepdims=True)
            e = jnp.exp(s - m)
            l = jnp.sum(e, axis=-1, keepdims=True)
            o = jnp.dot(e.astype(BF16), vbuf[pl.ds(off, 2 * WIN), pair(h)], preferred_element_type=F32) / l
            lse = jnp.broadcast_to(m + jnp.log(l), o.shape)
            if h % 2 == 0:
                o_pair, lse_pair = o, lse
            else:
                o_ref[0, pl.ds(off, WIN), pair(h)] = jnp.where(masks[0], o_pair, o).astype(BF16)
                lse_ref[0, pl.ds(off, WIN), pair(h)] = jnp.where(masks[0], lse_pair, lse)
        return carry

    lax.fori_loop(0, tq // WIN, body, 0)


def _attn_branch(q, k, v, bias, d, tq):
    b, l, _ = q.shape
    tq = min(tq, l)
    sub = tq // WIN
    cur = pl.BlockSpec((1, tq, A_W), lambda bi, r, i: (bi, i, r))
    halo = pl.BlockSpec((1, WIN, A_W), lambda bi, r, i: (bi, jnp.maximum(i * sub - 1, 0), r))
    return pl.pallas_call(
        functools.partial(_attn_kernel, tq=tq),
        grid=(b, d, l // tq),
        in_specs=[cur, halo, cur, halo, cur, _full((2, H_ATTN, WIN, 2 * WIN))],
        out_specs=[cur, cur],
        out_shape=[jax.ShapeDtypeStruct((b, l, d * A_W), BF16), jax.ShapeDtypeStruct((b, l, d * A_W), F32)],
        scratch_shapes=[pltpu.VMEM((tq + WIN, A_W), BF16), pltpu.VMEM((tq + WIN, A_W), BF16)],
        compiler_params=_params(("parallel", "parallel", "arbitrary")),
        name=f"attn_d{d}",
    )(q, k, k, v, v, bias)


def _t5_bucket(dist):
    dist = np.maximum(dist, 0)
    max_exact = NUM_BUCKETS // 2
    large = max_exact + (np.log(np.maximum(dist, 1) / max_exact)
                         / math.log(MAX_DISTANCE / max_exact)
                         * (NUM_BUCKETS - max_exact)).astype(np.int32)
    large = np.minimum(large, NUM_BUCKETS - 1)
    return np.where(dist < max_exact, dist, large).astype(np.int32)


def _bucket_bias(bias_table, dist):
    onehot = (_t5_bucket(dist).reshape(-1, 1) == np.arange(NUM_BUCKETS)[None, :])
    rows = jnp.dot(jnp.asarray(onehot, bias_table.dtype), bias_table, precision=lax.Precision.HIGHEST)
    return rows.T.reshape((bias_table.shape[1],) + dist.shape).astype(F32)


def _branch_bias(bias_table, d):
    m = np.arange(WIN)[:, None]
    n = np.arange(2 * WIN)[None, :]
    steps = m + WIN - n
    valid = (steps >= 0) & (steps <= WIN)
    bias = _bucket_bias(bias_table, steps * d)
    normal = jnp.where(jnp.asarray(valid), bias, NEG_INF)
    first = jnp.where(jnp.asarray(valid & (n >= WIN)), bias, NEG_INF)
    return jnp.stack([normal, first])


def _sample_attn_kernel(q_ref, kn_ref, vn_ref, kc_ref, vc_ref, bias_ref, o_ref, kbuf, vbuf, *, lw, tq):
    kbuf[0:lw, :] = kc_ref[0].astype(BF16)
    vbuf[0:lw, :] = vc_ref[0].astype(BF16)
    pad = jnp.zeros((kbuf.shape[0] - lw, A_W), BF16)
    kbuf[lw:, :] = pad
    vbuf[lw:, :] = pad
    kbuf[lw:lw + tq, :] = kn_ref[0].astype(BF16)
    vbuf[lw:lw + tq, :] = vn_ref[0].astype(BF16)
    masks = _head_masks()
    q = q_ref[0].astype(BF16)
    for pr in range(N_PAIRS):
        sl = slice(pr * PAIR, (pr + 1) * PAIR)
        q2 = q[:, sl]
        k2 = kbuf[:, sl]
        v2 = vbuf[:, sl]
        o_pair = None
        for hh in range(2):
            qm = jnp.where(masks[hh], q2, jnp.zeros_like(q2))
            s = _dot_nt(qm, k2) + bias_ref[2 * pr + hh]
            m = jnp.max(s, axis=-1, keepdims=True)
            e = jnp.exp(s - m)
            l = jnp.sum(e, axis=-1, keepdims=True)
            o = jnp.dot(e.astype(BF16), v2, preferred_element_type=F32) / l
            o_pair = o if hh == 0 else jnp.where(masks[0], o_pair, o)
        o_ref[0, :, sl] = o_pair


def _sample_bias(bias_table, lw, tq, nbuf):
    i = np.arange(tq)[:, None]
    row = np.arange(nbuf)[None, :]
    dist = lw + i - row
    live = (dist >= 0) & (row < lw + tq)
    count = np.zeros(dist.shape, np.int32)
    for w, d in BRANCHES:
        count += (live & (dist <= w) & (dist % d == 0)).astype(np.int32)
    bias = _bucket_bias(bias_table, dist)
    logc = jnp.log(jnp.asarray(np.maximum(count, 1)).astype(bias_table.dtype)).astype(F32)
    return jnp.where(jnp.asarray(count > 0), bias + logc, NEG_INF)


def _sample_attn(q, kn, vn, kc, vc, bias_table):
    b, tq, _ = q.shape
    lw = kc.shape[1]
    nbuf = lw + LANES
    new = pl.BlockSpec((1, tq, A_W), lambda bi: (bi, 0, 0))
    cache = pl.BlockSpec((1, lw, A_W), lambda bi: (bi, 0, 0))
    return pl.pallas_call(
        functools.partial(_sample_attn_kernel, lw=lw, tq=tq),
        grid=(b,),
        in_specs=[new, new, new, cache, cache, _full((H_ATTN, tq, nbuf))],
        out_specs=new,
        out_shape=jax.ShapeDtypeStruct((b, tq, A_W), F32),
        scratch_shapes=[pltpu.VMEM((nbuf, A_W), BF16), pltpu.VMEM((nbuf, A_W), BF16)],
        compiler_params=_params(("parallel",)),
        name="sample_attn",
    )(q, kn, vn, kc, vc, _sample_bias(bias_table, lw, tq, nbuf))


def _softplus(x):
    return jnp.maximum(x, 0.0) + jnp.log(1.0 + jnp.exp(-jnp.abs(x)))


def _rwkv_chunks(states, r, k, v, kk, a, cum, logw, last, c):
    tt = r.shape[0]
    nc = tt // c
    b = kk * a
    w_inv = jnp.exp(-cum)
    w_rem = jnp.exp(last - cum)
    rt = r * jnp.exp(cum)
    at = -kk * jnp.exp(cum - logw)
    bt = b * w_inv
    kt = k * w_inv
    bw = b * w_rem
    kw = k * w_rem
    w_end = jnp.exp(last)

    m0, m1 = _head_masks()
    lane2 = lax.broadcasted_iota(jnp.int32, (1, 2 * PAIR), 1)
    m0w = jnp.where(lane2 >= PAIR, lane2 - PAIR, lane2) < HEAD_DIM
    lane_c = lax.broadcasted_iota(jnp.int32, (c, 2 * c), 1)
    row_c = lax.broadcasted_iota(jnp.int32, (c, 2 * c), 0)
    left = lane_c < c
    col_c = jnp.where(left, lane_c, lane_c - c)
    strict = col_c < row_c
    incl = col_c <= row_c
    bd_c = ((lax.broadcasted_iota(jnp.int32, (2 * c, 2 * c), 0) >= c)
            == (lax.broadcasted_iota(jnp.int32, (2 * c, 2 * c), 1) >= c))
    bd_f = ((lax.broadcasted_iota(jnp.int32, (PAIR, PAIR), 0) >= HEAD_DIM)
            == (lax.broadcasted_iota(jnp.int32, (PAIR, PAIR), 1) >= HEAD_DIM))
    zeros = jnp.zeros((c, PAIR), BF16)

    items = [(n, pr) for n in range(nc) for pr in range(N_PAIRS)]
    cut = lambda x, it: x[it[0] * c:(it[0] + 1) * c, it[1] * PAIR:(it[1] + 1) * PAIR]
    sel = lambda m, x: jnp.where(m, x, jnp.zeros_like(x))
    cat0 = lambda *xs: jnp.concatenate(xs, axis=0)
    cat1 = lambda *xs: jnp.concatenate(xs, axis=1)

    def block_diag(x):
        xb = x.astype(BF16)
        return sel(bd_c, cat0(xb, xb))

    at_i = [cut(at, it).astype(BF16) for it in items]
    rt_i = [cut(rt, it) for it in items]
    v_i = [cut(v, it).astype(BF16) for it in items]
    bk = [cat0(cut(bt, it), cut(kt, it)).astype(BF16) for it in items]
    kb = [cat0(cut(kt, it), cut(bt, it)).astype(BF16) for it in items]
    rtb = [x.astype(BF16) for x in rt_i]
    aa0 = [_dot_nt(cat0(sel(m0, a_), sel(m0, r_)), y_) for a_, r_, y_ in zip(at_i, rtb, bk)]
    aa1 = [_dot_nt(cat0(sel(m1, a_), sel(m1, r_)), y_) for a_, r_, y_ in zip(at_i, rtb, kb)]
    x = [jnp.where(strict, jnp.where(left, p0[:c], p1[:c]), 0.0) for p0, p1 in zip(aa0, aa1)]
    akp = [jnp.where(strict, jnp.where(left, p1[:c], p0[:c]), 0.0) for p0, p1 in zip(aa0, aa1)]
    arb = [jnp.where(incl, jnp.where(left, p0[c:], p1[c:]), 0.0) for p0, p1 in zip(aa0, aa1)]
    ark = [jnp.where(incl, jnp.where(left, p1[c:], p0[c:]), 0.0) for p0, p1 in zip(aa0, aa1)]

    tp = x
    n = c
    while n > 2:
        xd = [block_diag(xi) for xi in x]
        x = [_dot(xi, d) for xi, d in zip(x, xd)]
        xd = [block_diag(xi) for xi in x]
        tp = [t + xi + _dot(t, d) for t, xi, d in zip(tp, x, xd)]
        n //= 2

    akv = [_dot(m_, cat0(sel(m1, v_), sel(m0, v_))) for m_, v_ in zip(akp, v_i)]
    rhs2 = [cat1(cut(at, it), q_) for it, q_ in zip(items, akv)]
    split = lambda x_: cat0(sel(m0w, x_), sel(~m0w, x_))
    pq = [x_ + _dot(t, split(x_.astype(BF16))) for x_, t in zip(rhs2, tp)]
    gy = [_dot(cat1(b_, k_), cat0(split(x_.astype(BF16)), cat1(zeros, sel(m1, v_)), cat1(zeros, sel(m0, v_))))
          for b_, k_, x_, v_ in zip(arb, ark, pq, v_i)]
    pg = [cat0(x_[:, :PAIR], r_ + g_[:, :PAIR]) for x_, r_, g_ in zip(pq, rt_i, gy)]

    states = list(states)
    y_rows = []
    for n in range(nc):
        idx = [n * N_PAIRS + pr for pr in range(N_PAIRS)]
        uy = [_dot_nt(pg[i], states[pr]) for pr, i in enumerate(idx)]
        u = [uy_[:c] + pq[i][:, PAIR:] for uy_, i in zip(uy, idx)]
        y_rows.append(cat1(*[uy_[c:] + gy[i][:, PAIR:] for uy_, i in zip(uy, idx)]))
        upd = [_dot_tn(cat0(u_.astype(BF16), v_i[i]), cat0(cut(bw, items[i]), cut(kw, items[i])))
               for u_, i in zip(u, idx)]
        states = [s * cut(w_end, items[i])[:1] + jnp.where(bd_f, d, 0.0) for s, d, i in zip(states, upd, idx)]
    return cat0(*y_rows), states


def _rwkv_kernel(p_ref, shift0_ref, s0_ref, mu_ref, w0_ref, wl_ref, a0_ref, al_ref, gl_ref,
                 kk_ref, ka_ref, rk_ref, lng_ref, lnb_ref, bd1_ref, bdm_ref,
                 y_ref, s_ref, carry, *, c):
    ci = pl.program_id(1)
    tt = p_ref.shape[1]

    @pl.when(ci == 0)
    def _():
        carry[...] = shift0_ref[0]
        s_ref[0] = s0_ref[0]

    p = p_ref[0]
    row = lax.broadcasted_iota(jnp.int32, (tt, 1), 0)
    prev = jnp.where(row == 0, carry[...], pltpu.roll(p, 1, 0))
    carry[...] = p[tt - 1:tt, :]
    xs = p + (prev - p) * mu_ref[...]
    r = xs[:, :B_W]
    k = xs[:, B_W:2 * B_W]
    v = xs[:, 2 * B_W:3 * B_W]
    xwa = xs[:, 3 * B_W:3 * B_W + LANES]
    xg = xs[:, 3 * B_W + LANES:]
    w_log = -_softplus(-(w0_ref[...] + _dot(jnp.tanh(xwa), wl_ref[...]))) - 0.5
    logw = -jnp.exp(w_log)
    a = jax.nn.sigmoid(a0_ref[...] + _dot(xwa, al_ref[...]))
    g = _dot(jax.nn.sigmoid(xg), gl_ref[...])
    bd1 = bd1_ref[...]
    kk = k * kk_ref[...]
    kk = kk / jnp.maximum(jnp.sqrt(_group_reduce(kk * kk, bd1)), 1e-12)
    k = k * (1.0 + (a - 1.0) * ka_ref[...])

    row_t = lax.broadcasted_iota(jnp.int32, (tt, tt), 0)
    col_t = lax.broadcasted_iota(jnp.int32, (tt, tt), 1)
    chunk_start = (row_t // c) * c
    tri = jnp.where((col_t <= row_t) & (col_t >= chunk_start), 1.0, 0.0).astype(BF16)
    hi, mid, lo = _split3(logw)
    cum = (jnp.dot(tri, hi, preferred_element_type=F32) + jnp.dot(tri, mid, preferred_element_type=F32)
           + jnp.dot(tri, lo, preferred_element_type=F32))
    cum3 = cum.reshape(tt // c, c, B_W)
    last = jnp.broadcast_to(cum3[:, c - 1:c, :], cum3.shape).reshape(tt, B_W)

    y, states = _rwkv_chunks([s_ref[0, pr] for pr in range(N_PAIRS)], r, k, v, kk, a, cum, logw, last, c)
    for pr in range(N_PAIRS):
        s_ref[0, pr] = states[pr]

    bdm = bdm_ref[...]
    dev = y - _group_reduce(y, bdm)
    yn = dev * lax.rsqrt(_group_reduce(dev * dev, bdm) + GN_EPS) * lng_ref[...] + lnb_ref[...]
    bonus = _group_reduce(r * k * rk_ref[...], bd1) * v
    y_ref[0] = ((yn + bonus) * g).astype(BF16)


def _rwkv(p, shift0, s0, weights, chunk, tile):
    b, t, _ = p.shape
    wspecs = [_full(w.shape) for w in weights]
    return pl.pallas_call(
        functools.partial(_rwkv_kernel, c=chunk),
        grid=(b, t // tile),
        in_specs=[pl.BlockSpec((1, tile, P_PAD), lambda bi, ci: (bi, ci, 0)),
                  pl.BlockSpec((1, 1, P_PAD), lambda bi, ci: (bi, 0, 0)),
                  pl.BlockSpec((1, N_PAIRS, PAIR, PAIR), lambda bi, ci: (bi, 0, 0, 0))] + wspecs,
        out_specs=[pl.BlockSpec((1, tile, B_W), lambda bi, ci: (bi, ci, 0)),
                   pl.BlockSpec((1, N_PAIRS, PAIR, PAIR), lambda bi, ci: (bi, 0, 0, 0))],
        out_shape=[jax.ShapeDtypeStruct((b, t, B_W), BF16),
                   jax.ShapeDtypeStruct((b, N_PAIRS, PAIR, PAIR), F32)],
        scratch_shapes=[pltpu.VMEM((1, P_PAD), F32)],
        compiler_params=_params(("parallel", "arbitrary")),
        name="rwkv7",
    )(p, shift0, s0, *weights)


def _pair_states(s):
    b = s.shape[0]
    s = s.reshape(b, N_PAIRS, 2, HEAD_DIM, HEAD_DIM)
    z = jnp.zeros_like(s[:, :, 0])
    top = jnp.concatenate([s[:, :, 0], z], axis=-1)
    bot = jnp.concatenate([z, s[:, :, 1]], axis=-1)
    return jnp.concatenate([top, bot], axis=-2)


def _unpair_states(s):
    return jnp.stack([s[:, :, :HEAD_DIM, :HEAD_DIM], s[:, :, HEAD_DIM:, HEAD_DIM:]], axis=2).reshape(
        s.shape[0], H_RWKV, HEAD_DIM, HEAD_DIM)


def _outmlp_kernel(*refs, dilations):
    n_branch = len(dilations)
    x_ref = refs[0]
    o_refs = refs[1:1 + n_branch]
    lse_refs = refs[1 + n_branch:1 + 2 * n_branch] if n_branch > 1 else ()
    n_in = 1 + n_branch + len(lse_refs)
    rw_ref, wo_ref, g2_ref, w1_ref, w2_ref, y_ref = refs[n_in:n_in + 6]
    stage = refs[n_in + 6:]
    tm = x_ref.shape[0]

    def natural(ref, d, st):
        if d == 1:
            return ref[...].astype(F32)
        for r in range(d):
            for j in range(A_W // LANES):
                lo = r * A_W + j * LANES
                st[j, pl.ds(r, tm // d, stride=d), :] = ref[:, lo:lo + LANES].astype(F32)
        return jnp.concatenate([st[j] for j in range(A_W // LANES)], axis=1)

    if n_branch > 1:
        lses = [natural(ref, d, stage[2 * i]) for i, (ref, d) in enumerate(zip(lse_refs, dilations))]
        top = functools.reduce(jnp.maximum, lses)
        num = den = None
        for i, (o_ref, d, lse) in enumerate(zip(o_refs, dilations, lses)):
            e = jnp.exp(lse - top)
            t = e * natural(o_ref, d, stage[2 * i + 1])
            num = t if num is None else num + t
            den = e if den is None else den + e
        att = num / den
    else:
        att = o_refs[0][...]
    h = x_ref[...] + _dot(att, wo_ref[:A_W, :]) + jnp.dot(rw_ref[...], wo_ref[A_W:, :],
                                                           preferred_element_type=F32)
    ms = jnp.mean(h * h, axis=-1, keepdims=True)
    m = (h * lax.rsqrt(ms + RMS_EPS) * g2_ref[...]).astype(BF16)
    acc = None
    for c in range(D_FF // D_MODEL):
        sl = slice(c * D_MODEL, (c + 1) * D_MODEL)
        u = jnp.maximum(jnp.dot(m, w1_ref[:, sl], preferred_element_type=F32), 0.0)
        t = jnp.dot((u * u).astype(BF16), w2_ref[sl, :], preferred_element_type=F32)
        acc = t if acc is None else acc + t
    y_ref[...] = h + acc


def _outmlp(x, os_, lses, dilations, rw, wo, g2, w1, w2, tm):
    n_tok = x.shape[0]
    tm = min(tm, n_tok)
    row = lambda w: pl.BlockSpec((tm, w), lambda i: (i, 0))
    view = lambda d: pl.BlockSpec((tm // d, d * A_W), lambda i: (i, 0))
    const = lambda shape: pl.BlockSpec(shape, lambda i: (0, 0), pipeline_mode=pl.Buffered(1))
    n_stage = 2 * len(dilations) if len(dilations) > 1 else 0
    return pl.pallas_call(
        functools.partial(_outmlp_kernel, dilations=dilations),
        grid=(n_tok // tm,),
        in_specs=[row(D_MODEL)] + [view(d) for d in dilations] + [view(d) for d in dilations[:len(lses)]]
        + [row(B_W), const((D_MODEL, D_MODEL)), const((1, D_MODEL)), const((D_MODEL, D_FF)),
           const((D_FF, D_MODEL))],
        out_specs=row(D_MODEL),
        out_shape=jax.ShapeDtypeStruct((n_tok, D_MODEL), F32),
        scratch_shapes=[pltpu.VMEM((A_W // LANES, tm, LANES), F32)] * n_stage,
        compiler_params=_params(("parallel",)),
        name="outmlp",
    )(x, *os_, *lses, rw, wo, g2, w1, w2)


def _row(v, width=None):
    v = v.reshape(1, -1).astype(F32)
    if width is not None and v.shape[1] < width:
        v = jnp.pad(v, ((0, 0), (0, width - v.shape[1])))
    return v


def _tile_heads(g, n_heads):
    return jnp.tile(g.reshape(1, HEAD_DIM).astype(F32), (1, n_heads))


def kernel(x_prompt, x_sample, cache_k_win, cache_v_win, state_wkv, state_shift, bias_table, ln1_g, w_in,
           q_norm_g, k_norm_g, mu_shift, w0, w_lora2, a0, a_lora2, g_lora2, k_k, k_a, r_k, lnx_g, lnx_b,
           w_out, ln2_g, w_mlp1, w_mlp2):
    depth = w_in.shape[0]
    assert depth == 1, "a deeper stack would repeat the per-layer calls below"
    li = 0
    bp, tp, _ = x_prompt.shape
    bs, ts, _ = x_sample.shape
    keep = min(MAX_WINDOW, tp)

    wqkv = w_in[li][:, :3 * A_W].astype(BF16)
    wp = jnp.pad(w_in[li][:, 3 * A_W:], ((0, 0), (0, P_PAD - N_SHIFT))).astype(BF16)
    gq = _tile_heads(q_norm_g[li], H_ATTN)
    gk = _tile_heads(k_norm_g[li], H_ATTN)
    zrow = lambda n: jnp.zeros((n, B_W), F32)
    rwkv_weights = [
        _row(mu_shift[li], P_PAD), _row(w0[li]),
        jnp.concatenate([w_lora2[li].astype(F32), zrow(LANES - LORA_W)], axis=0).astype(BF16),
        _row(a0[li]),
        jnp.concatenate([zrow(LORA_W), a_lora2[li].astype(F32)], axis=0).astype(BF16),
        jnp.concatenate([g_lora2[li].astype(F32), zrow(LORA_PAD - LORA_G)], axis=0).astype(BF16),
        _row(k_k[li]), _row(k_a[li]), _row(r_k[li]), _row(lnx_g[li]), _row(lnx_b[li]),
        _block_diag(B_W, 1.0), _block_diag(B_W, 1.0 / HEAD_DIM),
    ]
    wo = w_out[li].astype(BF16)
    w1 = w_mlp1[li].astype(BF16)
    w2 = w_mlp2[li].astype(BF16)
    g1 = _row(ln1_g[li])
    g2 = _row(ln2_g[li])

    xp = x_prompt.reshape(bp * tp, D_MODEL)
    dils = tuple(d for _, d in BRANCHES)
    q, k, v, kf, vf, p, *views = _inproj(xp, g1, wqkv, wp, gq, gk, tm=512, dilations=dils[1:])
    views = [q, k, v] + views
    os_, lses = [], []
    for i, d in enumerate(dils):
        qd, kd, vd = (a.reshape(bp, tp // d, d * A_W) for a in views[3 * i:3 * i + 3])
        o, lse = _attn_branch(qd, kd, vd, _branch_bias(bias_table, d), d, tq=1024)
        os_.append(o.reshape(bp * tp // d, d * A_W))
        lses.append(lse.reshape(bp * tp // d, d * A_W))
    p3 = p.reshape(bp, tp, P_PAD)
    rw, s_p = _rwkv(p3, jnp.zeros((bp, 1, P_PAD), F32), jnp.zeros((bp, N_PAIRS, PAIR, PAIR), F32),
                    rwkv_weights, chunk=64, tile=256)
    y_p = _outmlp(xp, os_, lses, dils, rw.reshape(bp * tp, B_W), wo, g2, w1, w2, tm=512)
    heads = lambda a, b_, t_: a.reshape(b_, t_, H_ATTN, HEAD_DIM)
    k_win_p = heads(kf, bp, tp)[:, tp - keep:][None]
    v_win_p = heads(vf, bp, tp)[:, tp - keep:][None]
    wkv_p = _unpair_states(s_p)[None]
    shift_p = p3[:, -1, :N_SHIFT][None]

    xs = x_sample.reshape(bs * ts, D_MODEL)
    qs, _, _, kfs, vfs, ps = _inproj(xs, g1, wqkv, wp, gq, gk, tm=512)
    lw = cache_k_win.shape[2]
    att_s = _sample_attn(qs.reshape(bs, ts, A_W).astype(F32), kfs.reshape(bs, ts, A_W), vfs.reshape(bs, ts, A_W),
                         cache_k_win[li].reshape(bs, lw, A_W), cache_v_win[li].reshape(bs, lw, A_W), bias_table)
    ps3 = ps.reshape(bs, ts, P_PAD)
    shift0 = jnp.pad(state_shift[li].astype(F32), ((0, 0), (0, P_PAD - N_SHIFT))).reshape(bs, 1, P_PAD)
    rw_s, s_s = _rwkv(ps3, shift0, _pair_states(state_wkv[li].astype(F32)), rwkv_weights, chunk=ts, tile=ts)
    y_s = _outmlp(xs, [att_s.reshape(bs * ts, A_W)], [], (1,), rw_s.reshape(bs * ts, B_W), wo, g2, w1, w2, tm=256)

    return (y_p.reshape(bp, tp, D_MODEL), y_s.reshape(bs, ts, D_MODEL), k_win_p, v_win_p, wkv_p, shift_p,
            heads(kfs, bs, ts)[None], heads(vfs, bs, ts)[None], _unpair_states(s_s)[None],
            ps3[:, -1, :N_SHIFT][None])
```

```python
import functools
import math

import numpy as np
import jax
import jax.numpy as jnp
from jax import lax
from jax.experimental import pallas as pl
from jax.experimental.pallas import tpu as pltpu

F32 = jnp.float32
BF16 = jnp.bfloat16

D_MODEL = 1024
HEAD_DIM = 64
A_W = 512
B_W = 512
H_ATTN = A_W // HEAD_DIM
H_RWKV = B_W // HEAD_DIM
BRANCHES = ((128, 1), (512, 4), (2048, 16))
MAX_WINDOW = 2048
WIN = 128
NUM_BUCKETS = 32
MAX_DISTANCE = 2048
LORA_W = 64
LORA_A = 64
LORA_G = 160
N_SHIFT = 3 * B_W + LORA_W + LORA_A + LORA_G
D_FF = 4 * D_MODEL
RMS_EPS = 1e-6
GN_EPS = 64e-5
NEG_INF = -1e30

LANES = 128
PAIR = 2 * HEAD_DIM
N_PAIRS = B_W // PAIR
P_PAD = 1920
LORA_PAD = P_PAD - 3 * B_W - LANES
VMEM_LIMIT = 56 * 1024 * 1024
SCORES_AHEAD = 8


def _dot(a, b):
    return jnp.dot(a.astype(BF16), b.astype(BF16), preferred_element_type=F32)


def _dot_nt(a, b):
    return lax.dot_general(a.astype(BF16), b.astype(BF16), (((1,), (1,)), ((), ())),
                           preferred_element_type=F32)


def _dot_tn(a, b):
    return lax.dot_general(a.astype(BF16), b.astype(BF16), (((0,), (0,)), ((), ())),
                           preferred_element_type=F32)


def _split2(x):
    hi = x.astype(BF16)
    lo = (x - hi.astype(F32)).astype(BF16)
    return hi, lo


def _split3(x):
    hi = x.astype(BF16)
    r1 = x - hi.astype(F32)
    mid = r1.astype(BF16)
    lo = (r1 - mid.astype(F32)).astype(BF16)
    return hi, mid, lo


def _group_reduce(x, bd):
    hi, lo = _split2(x)
    return jnp.dot(hi, bd, preferred_element_type=F32) + jnp.dot(lo, bd, preferred_element_type=F32)


def _block_diag(n, value):
    idx = np.arange(n) // HEAD_DIM
    return jnp.asarray(np.where(idx[:, None] == idx[None, :], value, 0.0), BF16)


def _full(shape):
    return pl.BlockSpec(shape, lambda *_: (0,) * len(shape))


def _params(sem):
    return pltpu.CompilerParams(dimension_semantics=sem, vmem_limit_bytes=VMEM_LIMIT)


def _inproj_kernel(x_ref, g1_ref, wqkv_ref, wp_ref, gq_ref, gk_ref, bd_ref, *refs, dilations):
    n_view = 3 * len(dilations)
    q_ref, k_ref, v_ref, kf_ref, vf_ref, p_ref = refs[:6]
    view_refs = refs[6:6 + n_view]
    stage = refs[6 + n_view:]
    x = x_ref[...]
    ms = jnp.mean(x * x, axis=-1, keepdims=True)
    n = (x * lax.rsqrt(ms + RMS_EPS) * g1_ref[...]).astype(BF16)
    p_ref[...] = jnp.dot(n, wp_ref[...], preferred_element_type=F32)
    qkv = jnp.dot(n, wqkv_ref[...], preferred_element_type=F32)
    q = qkv[:, :A_W]
    k = qkv[:, A_W:2 * A_W]
    v = qkv[:, 2 * A_W:]
    bd = bd_ref[...]
    qn = q * lax.rsqrt(_group_reduce(q * q, bd) + RMS_EPS) * gq_ref[...]
    kn = k * lax.rsqrt(_group_reduce(k * k, bd) + RMS_EPS) * gk_ref[...]
    qs = qn * (HEAD_DIM ** -0.5)
    q_ref[...] = qs.astype(BF16)
    k_ref[...] = kn.astype(BF16)
    v_ref[...] = v.astype(BF16)
    kf_ref[...] = kn
    vf_ref[...] = v
    if dilations:
        tm = x.shape[0]
        for a, st in zip((qs, kn, v), stage):
            for j in range(A_W // LANES):
                st[j] = a[:, j * LANES:(j + 1) * LANES]
        for di, d in enumerate(dilations):
            for ai, st in enumerate(stage):
                out = view_refs[3 * di + ai]
                for r in range(d):
                    for j in range(A_W // LANES):
                        lo = r * A_W + j * LANES
                        out[:, lo:lo + LANES] = st[j, pl.ds(r, tm // d, stride=d), :].astype(BF16)


def _inproj(x, g1, wqkv, wp, gq, gk, tm, seq=None, keep=None, dilations=()):
    n_tok = x.shape[0]
    tm = min(tm, n_tok)
    seq = n_tok if seq is None else seq
    keep = seq if keep is None else keep
    tiles, kept, skip = seq // tm, keep // tm, (seq - keep) // tm
    row = lambda w: pl.BlockSpec((tm, w), lambda i: (i, 0))
    tail = pl.BlockSpec((tm, A_W), lambda i: ((i // tiles) * kept + jnp.maximum(i % tiles - skip, 0), 0))
    view_specs = [pl.BlockSpec((tm // d, d * A_W), lambda i: (i, 0)) for d in dilations for _ in range(3)]
    view_shapes = [jax.ShapeDtypeStruct((n_tok // d, d * A_W), BF16) for d in dilations for _ in range(3)]
    return pl.pallas_call(
        functools.partial(_inproj_kernel, dilations=dilations),
        grid=(n_tok // tm,),
        in_specs=[row(D_MODEL), _full((1, D_MODEL)), _full((D_MODEL, 3 * A_W)), _full((D_MODEL, P_PAD)),
                  _full((1, A_W)), _full((1, A_W)), _full((A_W, A_W))],
        out_specs=[row(A_W), row(A_W), row(A_W), tail, tail, row(P_PAD)] + view_specs,
        out_shape=[jax.ShapeDtypeStruct((n_tok, A_W), BF16)] * 3
        + [jax.ShapeDtypeStruct((n_tok // seq * keep, A_W), F32)] * 2
        + [jax.ShapeDtypeStruct((n_tok, P_PAD), F32)] + view_shapes,
        scratch_shapes=[pltpu.VMEM((A_W // LANES, tm, LANES), F32)] * (3 if dilations else 0),
        compiler_params=_params(("arbitrary",)),
        name="inproj",
    )(x, g1, wqkv, wp, gq, gk, _block_diag(A_W, 1.0 / HEAD_DIM))


def _head_masks():
    lane = lax.broadcasted_iota(jnp.int32, (1, PAIR), 1)
    return lane < HEAD_DIM, lane >= HEAD_DIM


def _attn_kernel(q_ref, kh_ref, kc_ref, vh_ref, vc_ref, bias_ref, o_ref, lse_ref, kbuf, vbuf, *, tq):
    i = pl.program_id(2)
    kbuf[0:WIN, :] = kh_ref[0]
    kbuf[WIN:, :] = kc_ref[0]
    vbuf[0:WIN, :] = vh_ref[0]
    vbuf[WIN:, :] = vc_ref[0]
    masks = _head_masks()

    def body(j, carry):
        off = pl.multiple_of(j * WIN, WIN)
        first = jnp.logical_and(i == 0, j == 0).astype(jnp.int32)
        pair = lambda h: slice((h // 2) * PAIR, (h // 2 + 1) * PAIR)

        def scores(h):
            q2 = q_ref[0, pl.ds(off, WIN), pair(h)]
            qm = jnp.where(masks[h % 2], q2, jnp.zeros_like(q2))
            return _dot_nt(qm, kbuf[pl.ds(off, 2 * WIN), pair(h)]) + bias_ref[first, h]

        ahead = SCORES_AHEAD
        pending = [scores(h) for h in range(ahead)]
        o_pair = lse_pair = None
        for h in range(H_ATTN):
            if h + ahead < H_ATTN:
                pending.append(scores(h + ahead))
            s = pending.pop(0)
            m = jnp.max(s, axis=-1, keepdims=True)
            e = jnp.exp(s - m)
            l = jnp.sum(e, axis=-1, keepdims=True)
            o = jnp.dot(e.astype(BF16), vbuf[pl.ds(off, 2 * WIN), pair(h)], preferred_element_type=F32) / l
            lse = jnp.broadcast_to(m + jnp.log(l), o.shape)
            if h % 2 == 0:
                o_pair, lse_pair = o, lse
            else:
                o_ref[0, pl.ds(off, WIN), pair(h)] = jnp.where(masks[0], o_pair, o).astype(BF16)
                lse_ref[0, pl.ds(off, WIN), pair(h)] = jnp.where(masks[0], lse_pair, lse)
        return carry

    lax.fori_loop(0, tq // WIN, body, 0)


def _attn_branch(q, k, v, bias, d, tq):
    b, l, _ = q.shape
    tq = min(tq, l)
    sub = tq // WIN
    cur = pl.BlockSpec((1, tq, A_W), lambda bi, r, i: (bi, i, r))
    halo = pl.BlockSpec((1, WIN, A_W), lambda bi, r, i: (bi, jnp.maximum(i * sub - 1, 0), r))
    return pl.pallas_call(
        functools.partial(_attn_kernel, tq=tq),
        grid=(b, d, l // tq),
        in_specs=[cur, halo, cur, halo, cur, _full((2, H_ATTN, WIN, 2 * WIN))],
        out_specs=[cur, cur],
        out_shape=[jax.ShapeDtypeStruct((b, l, d * A_W), BF16), jax.ShapeDtypeStruct((b, l, d * A_W), F32)],
        scratch_shapes=[pltpu.VMEM((tq + WIN, A_W), BF16), pltpu.VMEM((tq + WIN, A_W), BF16)],
        compiler_params=_params(("parallel", "parallel", "arbitrary")),
        name=f"attn_d{d}",
    )(q, k, k, v, v, bias)


def _t5_bucket(dist):
    dist = np.maximum(dist, 0)
    max_exact = NUM_BUCKETS // 2
    large = max_exact + (np.log(np.maximum(dist, 1) / max_exact)
                         / math.log(MAX_DISTANCE / max_exact)
                         * (NUM_BUCKETS - max_exact)).astype(np.int32)
    large = np.minimum(large, NUM_BUCKETS - 1)
    return np.where(dist < max_exact, dist, large).astype(np.int32)


def _bucket_bias(bias_table, dist):
    onehot = (_t5_bucket(dist).reshape(-1, 1) == np.arange(NUM_BUCKETS)[None, :])
    rows = jnp.dot(jnp.asarray(onehot, bias_table.dtype), bias_table, precision=lax.Precision.HIGHEST)
    return rows.T.reshape((bias_table.shape[1],) + dist.shape).astype(F32)


def _branch_bias(bias_table, d):
    m = np.arange(WIN)[:, None]
    n = np.arange(2 * WIN)[None, :]
    steps = m + WIN - n
    valid = (steps >= 0) & (steps <= WIN)
    bias = _bucket_bias(bias_table, steps * d)
    normal = jnp.where(jnp.asarray(valid), bias, NEG_INF)
    first = jnp.where(jnp.asarray(valid & (n >= WIN)), bias, NEG_INF)
    return jnp.stack([normal, first])


def _sample_attn_kernel(q_ref, kn_ref, vn_ref, kc_ref, vc_ref, bias_ref, biasn_ref, o_ref, m_sc, l_sc, acc_sc):
    ci = pl.program_id(1)

    @pl.when(ci == 0)
    def _():
        m_sc[...] = jnp.full(m_sc.shape, 2 * NEG_INF, F32)
        l_sc[...] = jnp.zeros(l_sc.shape, F32)
        acc_sc[...] = jnp.zeros(acc_sc.shape, F32)

    q = q_ref[0].astype(BF16)

    def update(k_all, v_all, bias):
        s = _dot_nt(q, k_all) + bias
        m_old = m_sc[...]
        m_new = jnp.maximum(m_old, jnp.max(s, axis=-1, keepdims=True))
        alpha = jnp.exp(m_old - m_new)
        e = jnp.exp(s - m_new)
        l_sc[...] = alpha * l_sc[...] + jnp.sum(e, axis=-1, keepdims=True)
        acc_sc[...] = alpha * acc_sc[...] + jnp.dot(e.astype(BF16), v_all, preferred_element_type=F32)
        m_sc[...] = m_new

    rows = kc_ref.shape[1] * H_ATTN
    update(kc_ref[0].reshape(rows, HEAD_DIM).astype(BF16), vc_ref[0].reshape(rows, HEAD_DIM).astype(BF16),
           bias_ref[...])

    @pl.when(ci == pl.num_programs(1) - 1)
    def _():
        update(kn_ref[0].astype(BF16), vn_ref[0].astype(BF16), biasn_ref[...])
        o_ref[0] = acc_sc[...] / l_sc[...]


def _sample_bias(bias_table, lw, tq):
    i = np.arange(tq)[:, None]
    pos = np.arange(lw + tq)[None, :]
    dist = lw + i - pos
    count = np.zeros(dist.shape, np.int32)
    for w, d in BRANCHES:
        count += ((dist >= 0) & (dist <= w) & (dist % d == 0)).astype(np.int32)
    bias = _bucket_bias(bias_table, dist)
    logc = jnp.log(jnp.asarray(np.maximum(count, 1)).astype(bias_table.dtype)).astype(F32)
    per_head = jnp.where(jnp.asarray(count > 0), bias + logc, NEG_INF)
    same = jnp.asarray(np.eye(H_ATTN, dtype=bool))[:, None, None, :]
    full = jnp.where(same, per_head[:, :, :, None], NEG_INF)
    full = full.reshape(H_ATTN * tq, (lw + tq) * H_ATTN)
    return full[:, :lw * H_ATTN], full[:, lw * H_ATTN:]


def _sample_attn(q, kn, vn, kc, vc, bias_table, chunk):
    b, rows, _ = q.shape
    tq = rows // H_ATTN
    lw = kc.shape[1]
    new = pl.BlockSpec((1, rows, HEAD_DIM), lambda bi, ci: (bi, 0, 0))
    cache = pl.BlockSpec((1, chunk, H_ATTN, HEAD_DIM), lambda bi, ci: (bi, ci, 0, 0))
    bias_c, bias_n = _sample_bias(bias_table, lw, tq)
    return pl.pallas_call(
        _sample_attn_kernel,
        grid=(b, lw // chunk),
        in_specs=[new, new, new, cache, cache,
                  pl.BlockSpec((rows, chunk * H_ATTN), lambda bi, ci: (0, ci)), _full((rows, rows))],
        out_specs=new,
        out_shape=jax.ShapeDtypeStruct((b, rows, HEAD_DIM), F32),
        scratch_shapes=[pltpu.VMEM((rows, 1), F32), pltpu.VMEM((rows, 1), F32), pltpu.VMEM((rows, HEAD_DIM), F32)],
        compiler_params=_params(("parallel", "arbitrary")),
        name="sample_attn",
    )(q, kn, vn, kc, vc, bias_c, bias_n)


def _softplus(x):
    return jnp.maximum(x, 0.0) + jnp.log(1.0 + jnp.exp(-jnp.abs(x)))


def _rwkv_chunks(states, r, k, v, kk, a, cum, logw, last, c):
    tt = r.shape[0]
    nc = tt // c
    b = kk * a
    w_inv = jnp.exp(-cum)
    w_rem = jnp.exp(last - cum)
    rt = r * jnp.exp(cum)
    at = -kk * jnp.exp(cum - logw)
    bt = b * w_inv
    kt = k * w_inv
    bw = b * w_rem
    kw = k * w_rem
    w_end = jnp.exp(last)

    m0, m1 = _head_masks()
    lane2 = lax.broadcasted_iota(jnp.int32, (1, 2 * PAIR), 1)
    m0w = jnp.where(lane2 >= PAIR, lane2 - PAIR, lane2) < HEAD_DIM
    lane_c = lax.broadcasted_iota(jnp.int32, (c, 2 * c), 1)
    row_c = lax.broadcasted_iota(jnp.int32, (c, 2 * c), 0)
    left = lane_c < c
    col_c = jnp.where(left, lane_c, lane_c - c)
    strict = col_c < row_c
    incl = col_c <= row_c
    bd_c = ((lax.broadcasted_iota(jnp.int32, (2 * c, 2 * c), 0) >= c)
            == (lax.broadcasted_iota(jnp.int32, (2 * c, 2 * c), 1) >= c))
    bd_f = ((lax.broadcasted_iota(jnp.int32, (PAIR, PAIR), 0) >= HEAD_DIM)
            == (lax.broadcasted_iota(jnp.int32, (PAIR, PAIR), 1) >= HEAD_DIM))
    zeros = jnp.zeros((c, PAIR), BF16)

    items = [(n, pr) for n in range(nc) for pr in range(N_PAIRS)]
    cut = lambda x, it: x[it[0] * c:(it[0] + 1) * c, it[1] * PAIR:(it[1] + 1) * PAIR]
    sel = lambda m, x: jnp.where(m, x, jnp.zeros_like(x))
    cat0 = lambda *xs: jnp.concatenate(xs, axis=0)
    cat1 = lambda *xs: jnp.concatenate(xs, axis=1)

    def block_diag(x):
        xb = x.astype(BF16)
        return sel(bd_c, cat0(xb, xb))

    at_i = [cut(at, it).astype(BF16) for it in items]
    rt_i = [cut(rt, it) for it in items]
    v_i = [cut(v, it).astype(BF16) for it in items]
    bk = [cat0(cut(bt, it), cut(kt, it)).astype(BF16) for it in items]
    kb = [cat0(cut(kt, it), cut(bt, it)).astype(BF16) for it in items]
    rtb = [x.astype(BF16) for x in rt_i]
    aa0 = [_dot_nt(cat0(sel(m0, a_), sel(m0, r_)), y_) for a_, r_, y_ in zip(at_i, rtb, bk)]
    aa1 = [_dot_nt(cat0(sel(m1, a_), sel(m1, r_)), y_) for a_, r_, y_ in zip(at_i, rtb, kb)]
    x = [jnp.where(strict, jnp.where(left, p0[:c], p1[:c]), 0.0) for p0, p1 in zip(aa0, aa1)]
    akp = [jnp.where(strict, jnp.where(left, p1[:c], p0[:c]), 0.0) for p0, p1 in zip(aa0, aa1)]
    arb = [jnp.where(incl, jnp.where(left, p0[c:], p1[c:]), 0.0) for p0, p1 in zip(aa0, aa1)]
    ark = [jnp.where(incl, jnp.where(left, p1[c:], p0[c:]), 0.0) for p0, p1 in zip(aa0, aa1)]

    tp = x
    n = c
    while n > 2:
        xd = [block_diag(xi) for xi in x]
        x = [_dot(xi, d) for xi, d in zip(x, xd)]
        xd = [block_diag(xi) for xi in x]
        tp = [t + xi + _dot(t, d) for t, xi, d in zip(tp, x, xd)]
        n //= 2

    akv = [_dot(m_, cat0(sel(m1, v_), sel(m0, v_))) for m_, v_ in zip(akp, v_i)]
    rhs2 = [cat1(cut(at, it), q_) for it, q_ in zip(items, akv)]
    split = lambda x_: cat0(sel(m0w, x_), sel(~m0w, x_))
    pq = [x_ + _dot(t, split(x_.astype(BF16))) for x_, t in zip(rhs2, tp)]
    gy = [_dot(cat1(b_, k_), cat0(split(x_.astype(BF16)), cat1(zeros, sel(m1, v_)), cat1(zeros, sel(m0, v_))))
          for b_, k_, x_, v_ in zip(arb, ark, pq, v_i)]
    pg = [cat0(x_[:, :PAIR], r_ + g_[:, :PAIR]) for x_, r_, g_ in zip(pq, rt_i, gy)]

    states = list(states)
    y_rows = []
    for n in range(nc):
        idx = [n * N_PAIRS + pr for pr in range(N_PAIRS)]
        uy = [_dot_nt(pg[i], states[pr]) for pr, i in enumerate(idx)]
        u = [uy_[:c] + pq[i][:, PAIR:] for uy_, i in zip(uy, idx)]
        y_rows.append(cat1(*[uy_[c:] + gy[i][:, PAIR:] for uy_, i in zip(uy, idx)]))
        upd = [_dot_tn(cat0(u_.astype(BF16), v_i[i]), cat0(cut(bw, items[i]), cut(kw, items[i])))
               for u_, i in zip(u, idx)]
        states = [s * cut(w_end, items[i])[:1] + jnp.where(bd_f, d, 0.0) for s, d, i in zip(states, upd, idx)]
    return cat0(*y_rows), states


def _rwkv_kernel(p_ref, shift0_ref, s0_ref, mu_ref, w0_ref, wl_ref, a0_ref, al_ref, gl_ref,
                 kk_ref, ka_ref, rk_ref, lng_ref, lnb_ref, bd1_ref, bdm_ref,
                 y_ref, s_ref, carry, *, c):
    ci = pl.program_id(1)
    tt = p_ref.shape[1]

    @pl.when(ci == 0)
    def _():
        carry[...] = shift0_ref[0]
        s_ref[0] = s0_ref[0]

    p = p_ref[0]
    row = lax.broadcasted_iota(jnp.int32, (tt, 1), 0)
    prev = jnp.where(row == 0, carry[...], pltpu.roll(p, 1, 0))
    carry[...] = p[tt - 1:tt, :]
    xs = p + (prev - p) * mu_ref[...]
    r = xs[:, :B_W]
    k = xs[:, B_W:2 * B_W]
    v = xs[:, 2 * B_W:3 * B_W]
    xwa = xs[:, 3 * B_W:3 * B_W + LANES]
    xg = xs[:, 3 * B_W + LANES:]
    w_log = -_softplus(-(w0_ref[...] + _dot(jnp.tanh(xwa), wl_ref[...]))) - 0.5
    logw = -jnp.exp(w_log)
    a = jax.nn.sigmoid(a0_ref[...] + _dot(xwa, al_ref[...]))
    g = _dot(jax.nn.sigmoid(xg), gl_ref[...])
    bd1 = bd1_ref[...]
    kk = k * kk_ref[...]
    kk = kk / jnp.maximum(jnp.sqrt(_group_reduce(kk * kk, bd1)), 1e-12)
    k = k * (1.0 + (a - 1.0) * ka_ref[...])

    row_t = lax.broadcasted_iota(jnp.int32, (tt, tt), 0)
    col_t = lax.broadcasted_iota(jnp.int32, (tt, tt), 1)
    chunk_start = (row_t // c) * c
    tri = jnp.where((col_t <= row_t) & (col_t >= chunk_start), 1.0, 0.0).astype(BF16)
    hi, mid, lo = _split3(logw)
    cum = (jnp.dot(tri, hi, preferred_element_type=F32) + jnp.dot(tri, mid, preferred_element_type=F32)
           + jnp.dot(tri, lo, preferred_element_type=F32))
    cum3 = cum.reshape(tt // c, c, B_W)
    last = jnp.broadcast_to(cum3[:, c - 1:c, :], cum3.shape).reshape(tt, B_W)

    y, states = _rwkv_chunks([s_ref[0, pr] for pr in range(N_PAIRS)], r, k, v, kk, a, cum, logw, last, c)
    for pr in range(N_PAIRS):
        s_ref[0, pr] = states[pr]

    bdm = bdm_ref[...]
    dev = y - _group_reduce(y, bdm)
    yn = dev * lax.rsqrt(_group_reduce(dev * dev, bdm) + GN_EPS) * lng_ref[...] + lnb_ref[...]
    bonus = _group_reduce(r * k * rk_ref[...], bd1) * v
    y_ref[0] = ((yn + bonus) * g).astype(BF16)


def _rwkv(p, shift0, s0, weights, chunk, tile):
    b, t, _ = p.shape
    wspecs = [_full(w.shape) for w in weights]
    return pl.pallas_call(
        functools.partial(_rwkv_kernel, c=chunk),
        grid=(b, t // tile),
        in_specs=[pl.BlockSpec((1, tile, P_PAD), lambda bi, ci: (bi, ci, 0)),
                  pl.BlockSpec((1, 1, P_PAD), lambda bi, ci: (bi, 0, 0)),
                  pl.BlockSpec((1, N_PAIRS, PAIR, PAIR), lambda bi, ci: (bi, 0, 0, 0))] + wspecs,
        out_specs=[pl.BlockSpec((1, tile, B_W), lambda bi, ci: (bi, ci, 0)),
                   pl.BlockSpec((1, N_PAIRS, PAIR, PAIR), lambda bi, ci: (bi, 0, 0, 0))],
        out_shape=[jax.ShapeDtypeStruct((b, t, B_W), BF16),
                   jax.ShapeDtypeStruct((b, N_PAIRS, PAIR, PAIR), F32)],
        scratch_shapes=[pltpu.VMEM((1, P_PAD), F32)],
        compiler_params=_params(("parallel", "arbitrary")),
        name="rwkv7",
    )(p, shift0, s0, *weights)


def _pair_states(s):
    b = s.shape[0]
    s = s.reshape(b, N_PAIRS, 2, HEAD_DIM, HEAD_DIM)
    z = jnp.zeros_like(s[:, :, 0])
    top = jnp.concatenate([s[:, :, 0], z], axis=-1)
    bot = jnp.concatenate([z, s[:, :, 1]], axis=-1)
    return jnp.concatenate([top, bot], axis=-2)


def _unpair_states(s):
    return jnp.stack([s[:, :, :HEAD_DIM, :HEAD_DIM], s[:, :, HEAD_DIM:, HEAD_DIM:]], axis=2).reshape(
        s.shape[0], H_RWKV, HEAD_DIM, HEAD_DIM)


def _outmlp_kernel(*refs, dilations):
    n_branch = len(dilations)
    x_ref = refs[0]
    o_refs = refs[1:1 + n_branch]
    lse_refs = refs[1 + n_branch:1 + 2 * n_branch] if n_branch > 1 else ()
    n_in = 1 + n_branch + len(lse_refs)
    rw_ref, wo_ref, g2_ref, w1_ref, w2_ref, y_ref = refs[n_in:n_in + 6]
    stage = refs[n_in + 6:]
    tm = x_ref.shape[0]

    def natural(ref, d, st):
        if d == 1:
            return ref[...].astype(F32)
        for r in range(d):
            for j in range(A_W // LANES):
                lo = r * A_W + j * LANES
                st[j, pl.ds(r, tm // d, stride=d), :] = ref[:, lo:lo + LANES].astype(F32)
        return jnp.concatenate([st[j] for j in range(A_W // LANES)], axis=1)

    if n_branch > 1:
        lses = [natural(ref, d, stage[2 * i]) for i, (ref, d) in enumerate(zip(lse_refs, dilations))]
        top = functools.reduce(jnp.maximum, lses)
        num = den = None
        for i, (o_ref, d, lse) in enumerate(zip(o_refs, dilations, lses)):
            e = jnp.exp(lse - top)
            t = e * natural(o_ref, d, stage[2 * i + 1])
            num = t if num is None else num + t
            den = e if den is None else den + e
        att = num / den
    else:
        att = o_refs[0][...]
    h = x_ref[...] + _dot(att, wo_ref[:A_W, :]) + jnp.dot(rw_ref[...], wo_ref[A_W:, :],
                                                           preferred_element_type=F32)
    ms = jnp.mean(h * h, axis=-1, keepdims=True)
    m = (h * lax.rsqrt(ms + RMS_EPS) * g2_ref[...]).astype(BF16)
    acc = None
    for c in range(D_FF // D_MODEL):
        sl = slice(c * D_MODEL, (c + 1) * D_MODEL)
        u = jnp.maximum(jnp.dot(m, w1_ref[:, sl], preferred_element_type=F32), 0.0)
        t = jnp.dot((u * u).astype(BF16), w2_ref[sl, :], preferred_element_type=F32)
        acc = t if acc is None else acc + t
    y_ref[...] = h + acc


def _outmlp(x, os_, lses, dilations, rw, wo, g2, w1, w2, tm):
    n_tok = x.shape[0]
    tm = min(tm, n_tok)
    row = lambda w: pl.BlockSpec((tm, w), lambda i: (i, 0))
    view = lambda d: pl.BlockSpec((tm // d, d * A_W), lambda i: (i, 0))
    const = lambda shape: pl.BlockSpec(shape, lambda i: (0, 0), pipeline_mode=pl.Buffered(1))
    n_stage = 2 * len(dilations) if len(dilations) > 1 else 0
    return pl.pallas_call(
        functools.partial(_outmlp_kernel, dilations=dilations),
        grid=(n_tok // tm,),
        in_specs=[row(D_MODEL)] + [view(d) for d in dilations] + [view(d) for d in dilations[:len(lses)]]
        + [row(B_W), const((D_MODEL, D_MODEL)), const((1, D_MODEL)), const((D_MODEL, D_FF)),
           const((D_FF, D_MODEL))],
        out_specs=row(D_MODEL),
        out_shape=jax.ShapeDtypeStruct((n_tok, D_MODEL), F32),
        scratch_shapes=[pltpu.VMEM((A_W // LANES, tm, LANES), F32)] * n_stage,
        compiler_params=_params(("parallel",)),
        name="outmlp",
    )(x, *os_, *lses, rw, wo, g2, w1, w2)


def _row(v, width=None):
    v = v.reshape(1, -1).astype(F32)
    if width is not None and v.shape[1] < width:
        v = jnp.pad(v, ((0, 0), (0, width - v.shape[1])))
    return v


def _tile_heads(g, n_heads):
    return jnp.tile(g.reshape(1, HEAD_DIM).astype(F32), (1, n_heads))


def kernel(x_prompt, x_sample, cache_k_win, cache_v_win, state_wkv, state_shift, bias_table, ln1_g, w_in,
           q_norm_g, k_norm_g, mu_shift, w0, w_lora2, a0, a_lora2, g_lora2, k_k, k_a, r_k, lnx_g, lnx_b,
           w_out, ln2_g, w_mlp1, w_mlp2):
    depth = w_in.shape[0]
    assert depth == 1, "a deeper stack would repeat the per-layer calls below"
    li = 0
    bp, tp, _ = x_prompt.shape
    bs, ts, _ = x_sample.shape
    keep = min(MAX_WINDOW, tp)

    wqkv = w_in[li][:, :3 * A_W].astype(BF16)
    wp = jnp.pad(w_in[li][:, 3 * A_W:], ((0, 0), (0, P_PAD - N_SHIFT))).astype(BF16)
    gq = _tile_heads(q_norm_g[li], H_ATTN)
    gk = _tile_heads(k_norm_g[li], H_ATTN)
    zrow = lambda n: jnp.zeros((n, B_W), F32)
    rwkv_weights = [
        _row(mu_shift[li], P_PAD), _row(w0[li]),
        jnp.concatenate([w_lora2[li].astype(F32), zrow(LANES - LORA_W)], axis=0).astype(BF16),
        _row(a0[li]),
        jnp.concatenate([zrow(LORA_W), a_lora2[li].astype(F32)], axis=0).astype(BF16),
        jnp.concatenate([g_lora2[li].astype(F32), zrow(LORA_PAD - LORA_G)], axis=0).astype(BF16),
        _row(k_k[li]), _row(k_a[li]), _row(r_k[li]), _row(lnx_g[li]), _row(lnx_b[li]),
        _block_diag(B_W, 1.0), _block_diag(B_W, 1.0 / HEAD_DIM),
    ]
    wo = w_out[li].astype(BF16)
    w1 = w_mlp1[li].astype(BF16)
    w2 = w_mlp2[li].astype(BF16)
    g1 = _row(ln1_g[li])
    g2 = _row(ln2_g[li])

    xp = x_prompt.reshape(bp * tp, D_MODEL)
    dils = tuple(d for _, d in BRANCHES)
    q, k, v, kf, vf, p, *views = _inproj(xp, g1, wqkv, wp, gq, gk, tm=512, seq=tp, keep=keep,
                                         dilations=dils[1:])
    views = [q, k, v] + views
    os_, lses = [], []
    for i, d in enumerate(dils):
        qd, kd, vd = (a.reshape(bp, tp // d, d * A_W) for a in views[3 * i:3 * i + 3])
        o, lse = _attn_branch(qd, kd, vd, _branch_bias(bias_table, d), d, tq=1024)
        os_.append(o.reshape(bp * tp // d, d * A_W))
        lses.append(lse.reshape(bp * tp // d, d * A_W))
    p3 = p.reshape(bp, tp, P_PAD)
    rw, s_p = _rwkv(p3, jnp.zeros((bp, 1, P_PAD), F32), jnp.zeros((bp, N_PAIRS, PAIR, PAIR), F32),
                    rwkv_weights, chunk=64, tile=256)
    y_p = _outmlp(xp, os_, lses, dils, rw.reshape(bp * tp, B_W), wo, g2, w1, w2, tm=512)
    heads = lambda a, b_, t_: a.reshape(b_, t_, H_ATTN, HEAD_DIM)
    k_win_p = heads(kf, bp, keep)[None]
    v_win_p = heads(vf, bp, keep)[None]
    wkv_p = _unpair_states(s_p)[None]
    shift_p = p3[:, -1, :N_SHIFT][None]

    xs = x_sample.reshape(bs * ts, D_MODEL)
    qs, _, _, kfs, vfs, ps = _inproj(xs, g1, wqkv, wp, gq, gk, tm=512)
    to_head_rows = lambda a: a.reshape(bs, ts, H_ATTN, HEAD_DIM).transpose(0, 2, 1, 3).reshape(bs, H_ATTN * ts, HEAD_DIM)
    att_s = _sample_attn(to_head_rows(qs.astype(F32)), kfs.reshape(bs, ts * H_ATTN, HEAD_DIM),
                         vfs.reshape(bs, ts * H_ATTN, HEAD_DIM), cache_k_win[li], cache_v_win[li], bias_table,
                         chunk=1024)
    att_s = att_s.reshape(bs, H_ATTN, ts, HEAD_DIM).transpose(0, 2, 1, 3)
    ps3 = ps.reshape(bs, ts, P_PAD)
    shift0 = jnp.pad(state_shift[li].astype(F32), ((0, 0), (0, P_PAD - N_SHIFT))).reshape(bs, 1, P_PAD)
    rw_s, s_s = _rwkv(ps3, shift0, _pair_states(state_wkv[li].astype(F32)), rwkv_weights, chunk=ts, tile=ts)
    y_s = _outmlp(xs, [att_s.reshape(bs * ts, A_W)], [], (1,), rw_s.reshape(bs * ts, B_W), wo, g2, w1, w2, tm=256)

    return (y_p.reshape(bp, tp, D_MODEL), y_s.reshape(bs, ts, D_MODEL), k_win_p, v_win_p, wkv_p, shift_p,
            heads(kfs, bs, ts)[None], heads(vfs, bs, ts)[None], _unpair_states(s_s)[None],
            ps3[:, -1, :N_SHIFT][None])
```

```python
import functools
import math

import numpy as np
import jax
import jax.numpy as jnp
from jax import lax
from jax.experimental import pallas as pl
from jax.experimental.pallas import tpu as pltpu

F32 = jnp.float32
BF16 = jnp.bfloat16

D_MODEL = 1024
HEAD_DIM = 64
A_W = 512
B_W = 512
H_ATTN = A_W // HEAD_DIM
H_RWKV = B_W // HEAD_DIM
BRANCHES = ((128, 1), (512, 4), (2048, 16))
MAX_WINDOW = 2048
WIN = 128
NUM_BUCKETS = 32
MAX_DISTANCE = 2048
LORA_W = 64
LORA_A = 64
LORA_G = 160
N_SHIFT = 3 * B_W + LORA_W + LORA_A + LORA_G
D_FF = 4 * D_MODEL
RMS_EPS = 1e-6
GN_EPS = 64e-5
NEG_INF = -1e30

LANES = 128
PAIR = 2 * HEAD_DIM
N_PAIRS = B_W // PAIR
P_PAD = 1920
LORA_PAD = P_PAD - 3 * B_W - LANES
VMEM_LIMIT = 56 * 1024 * 1024
SCORES_AHEAD = 8


def _dot(a, b):
    return jnp.dot(a.astype(BF16), b.astype(BF16), preferred_element_type=F32)


def _dot_nt(a, b):
    return lax.dot_general(a.astype(BF16), b.astype(BF16), (((1,), (1,)), ((), ())),
                           preferred_element_type=F32)


def _dot_tn(a, b):
    return lax.dot_general(a.astype(BF16), b.astype(BF16), (((0,), (0,)), ((), ())),
                           preferred_element_type=F32)


def _split2(x):
    hi = x.astype(BF16)
    lo = (x - hi.astype(F32)).astype(BF16)
    return hi, lo


def _split3(x):
    hi = x.astype(BF16)
    r1 = x - hi.astype(F32)
    mid = r1.astype(BF16)
    lo = (r1 - mid.astype(F32)).astype(BF16)
    return hi, mid, lo


def _group_reduce(x, bd):
    hi, lo = _split2(x)
    return jnp.dot(hi, bd, preferred_element_type=F32) + jnp.dot(lo, bd, preferred_element_type=F32)


def _block_diag(n, value):
    idx = np.arange(n) // HEAD_DIM
    return jnp.asarray(np.where(idx[:, None] == idx[None, :], value, 0.0), BF16)


def _full(shape):
    return pl.BlockSpec(shape, lambda *_: (0,) * len(shape))


def _params(sem):
    return pltpu.CompilerParams(dimension_semantics=sem, vmem_limit_bytes=VMEM_LIMIT)


def _inproj_kernel(x_ref, g1_ref, wqkv_ref, wp_ref, gq_ref, gk_ref, bd_ref, *refs, dilations):
    n_view = 3 * len(dilations)
    q_ref, k_ref, v_ref, kf_ref, vf_ref, p_ref = refs[:6]
    view_refs = refs[6:6 + n_view]
    stage = refs[6 + n_view:]
    x = x_ref[...]
    ms = jnp.mean(x * x, axis=-1, keepdims=True)
    n = (x * lax.rsqrt(ms + RMS_EPS) * g1_ref[...]).astype(BF16)
    p_ref[...] = jnp.dot(n, wp_ref[...], preferred_element_type=F32)
    qkv = jnp.dot(n, wqkv_ref[...], preferred_element_type=F32)
    q = qkv[:, :A_W]
    k = qkv[:, A_W:2 * A_W]
    v = qkv[:, 2 * A_W:]
    bd = bd_ref[...]
    qn = q * lax.rsqrt(_group_reduce(q * q, bd) + RMS_EPS) * gq_ref[...]
    kn = k * lax.rsqrt(_group_reduce(k * k, bd) + RMS_EPS) * gk_ref[...]
    qs = qn * (HEAD_DIM ** -0.5)
    q_ref[...] = qs.astype(BF16)
    k_ref[...] = kn.astype(BF16)
    v_ref[...] = v.astype(BF16)
    kf_ref[...] = kn
    vf_ref[...] = v
    if dilations:
        tm = x.shape[0]
        for a, st in zip((qs, kn, v), stage):
            for j in range(A_W // LANES):
                st[j] = a[:, j * LANES:(j + 1) * LANES]
        for di, d in enumerate(dilations):
            for ai, st in enumerate(stage):
                out = view_refs[3 * di + ai]
                for r in range(d):
                    for j in range(A_W // LANES):
                        lo = r * A_W + j * LANES
                        out[:, lo:lo + LANES] = st[j, pl.ds(r, tm // d, stride=d), :].astype(BF16)


def _inproj(x, g1, wqkv, wp, gq, gk, tm, seq=None, keep=None, dilations=()):
    n_tok = x.shape[0]
    tm = min(tm, n_tok)
    seq = n_tok if seq is None else seq
    keep = seq if keep is None else keep
    tiles, kept, skip = seq // tm, keep // tm, (seq - keep) // tm
    row = lambda w: pl.BlockSpec((tm, w), lambda i: (i, 0))
    tail = pl.BlockSpec((tm, A_W), lambda i: ((i // tiles) * kept + jnp.maximum(i % tiles - skip, 0), 0))
    view_specs = [pl.BlockSpec((tm // d, d * A_W), lambda i: (i, 0)) for d in dilations for _ in range(3)]
    view_shapes = [jax.ShapeDtypeStruct((n_tok // d, d * A_W), BF16) for d in dilations for _ in range(3)]
    return pl.pallas_call(
        functools.partial(_inproj_kernel, dilations=dilations),
        grid=(n_tok // tm,),
        in_specs=[row(D_MODEL), _full((1, D_MODEL)), _full((D_MODEL, 3 * A_W)), _full((D_MODEL, P_PAD)),
                  _full((1, A_W)), _full((1, A_W)), _full((A_W, A_W))],
        out_specs=[row(A_W), row(A_W), row(A_W), tail, tail, row(P_PAD)] + view_specs,
        out_shape=[jax.ShapeDtypeStruct((n_tok, A_W), BF16)] * 3
        + [jax.ShapeDtypeStruct((n_tok // seq * keep, A_W), F32)] * 2
        + [jax.ShapeDtypeStruct((n_tok, P_PAD), F32)] + view_shapes,
        scratch_shapes=[pltpu.VMEM((A_W // LANES, tm, LANES), F32)] * (3 if dilations else 0),
        compiler_params=_params(("arbitrary",)),
        name="inproj",
    )(x, g1, wqkv, wp, gq, gk, _block_diag(A_W, 1.0 / HEAD_DIM))


def _head_masks():
    lane = lax.broadcasted_iota(jnp.int32, (1, PAIR), 1)
    return lane < HEAD_DIM, lane >= HEAD_DIM


def _attn_kernel(q_ref, kh_ref, kc_ref, vh_ref, vc_ref, bias_ref, o_ref, lse_ref, kbuf, vbuf, *, tq):
    i = pl.program_id(2)
    kbuf[0:WIN, :] = kh_ref[0]
    kbuf[WIN:, :] = kc_ref[0]
    vbuf[0:WIN, :] = vh_ref[0]
    vbuf[WIN:, :] = vc_ref[0]
    masks = _head_masks()

    def body(j, carry):
        off = pl.multiple_of(j * WIN, WIN)
        first = jnp.logical_and(i == 0, j == 0).astype(jnp.int32)
        pair = lambda h: slice((h // 2) * PAIR, (h // 2 + 1) * PAIR)

        def scores(h):
            q2 = q_ref[0, pl.ds(off, WIN), pair(h)]
            qm = jnp.where(masks[h % 2], q2, jnp.zeros_like(q2))
            return _dot_nt(qm, kbuf[pl.ds(off, 2 * WIN), pair(h)]) + bias_ref[first, h]

        ahead = SCORES_AHEAD
        pending = [scores(h) for h in range(ahead)]
        o_pair = lse_pair = None
        for h in range(H_ATTN):
            if h + ahead < H_ATTN:
                pending.append(scores(h + ahead))
            s = pending.pop(0)
            m = jnp.max(s, axis=-1, keepdims=True)
            e = jnp.exp(s - m)
            l = jnp.sum(e, axis=-1, keepdims=True)
            o = jnp.dot(e.astype(BF16), vbuf[pl.ds(off, 2 * WIN), pair(h)], preferred_element_type=F32) / l
            lse = jnp.broadcast_to(m + jnp.log(l), o.shape)
            if h % 2 == 0:
                o_pair, lse_pair = o, lse
            else:
                o_ref[0, pl.ds(off, WIN), pair(h)] = jnp.where(masks[0], o_pair, o).astype(BF16)
                lse_ref[0, pl.ds(off, WIN), pair(h)] = jnp.where(masks[0], lse_pair, lse)
        return carry

    lax.fori_loop(0, tq // WIN, body, 0)


def _attn_branch(q, k, v, bias, d, tq):
    b, l, _ = q.shape
    tq = min(tq, l)
    sub = tq // WIN
    cur = pl.BlockSpec((1, tq, A_W), lambda bi, r, i: (bi, i, r))
    halo = pl.BlockSpec((1, WIN, A_W), lambda bi, r, i: (bi, jnp.maximum(i * sub - 1, 0), r))
    return pl.pallas_call(
        functools.partial(_attn_kernel, tq=tq),
        grid=(b, d, l // tq),
        in_specs=[cur, halo, cur, halo, cur, _full((2, H_ATTN, WIN, 2 * WIN))],
        out_specs=[cur, cur],
        out_shape=[jax.ShapeDtypeStruct((b, l, d * A_W), BF16), jax.ShapeDtypeStruct((b, l, d * A_W), F32)],
        scratch_shapes=[pltpu.VMEM((tq + WIN, A_W), BF16), pltpu.VMEM((tq + WIN, A_W), BF16)],
        compiler_params=_params(("parallel", "parallel", "arbitrary")),
        name=f"attn_d{d}",
    )(q, k, k, v, v, bias)


def _t5_bucket(dist):
    dist = np.maximum(dist, 0)
    max_exact = NUM_BUCKETS // 2
    large = max_exact + (np.log(np.maximum(dist, 1) / max_exact)
                         / math.log(MAX_DISTANCE / max_exact)
                         * (NUM_BUCKETS - max_exact)).astype(np.int32)
    large = np.minimum(large, NUM_BUCKETS - 1)
    return np.where(dist < max_exact, dist, large).astype(np.int32)


def _bucket_bias(bias_table, dist):
    onehot = (_t5_bucket(dist).reshape(-1, 1) == np.arange(NUM_BUCKETS)[None, :])
    rows = jnp.dot(jnp.asarray(onehot, bias_table.dtype), bias_table, precision=lax.Precision.HIGHEST)
    return rows.T.reshape((bias_table.shape[1],) + dist.shape).astype(F32)


def _branch_bias(bias_table, d):
    m = np.arange(WIN)[:, None]
    n = np.arange(2 * WIN)[None, :]
    steps = m + WIN - n
    valid = (steps >= 0) & (steps <= WIN)
    bias = _bucket_bias(bias_table, steps * d)
    normal = jnp.where(jnp.asarray(valid), bias, NEG_INF)
    first = jnp.where(jnp.asarray(valid & (n >= WIN)), bias, NEG_INF)
    return jnp.stack([normal, first])


def _sample_attn_kernel(q_ref, kn_ref, vn_ref, kc_ref, vc_ref, bias_ref, biasn_ref, o_ref):
    heads = range(H_ATTN)
    q = [q_ref[0, h].astype(BF16) for h in heads]
    s_c = [jnp.dot(q[h], kc_ref[0, h].astype(BF16), preferred_element_type=F32) + bias_ref[h] for h in heads]
    s_n = [_dot_nt(q[h], kn_ref[0, h]) + biasn_ref[h] for h in heads]
    for h in heads:
        m = jnp.maximum(jnp.max(s_c[h], axis=-1, keepdims=True), jnp.max(s_n[h], axis=-1, keepdims=True))
        e_c = jnp.exp(s_c[h] - m)
        e_n = jnp.exp(s_n[h] - m)
        l = jnp.sum(e_c, axis=-1, keepdims=True) + jnp.sum(e_n, axis=-1, keepdims=True)
        o = _dot_nt(e_c, vc_ref[0, h]) + _dot(e_n, vn_ref[0, h])
        o_ref[0, h] = o / l


def _sample_bias(bias_table, lw, tq):
    i = np.arange(tq)[:, None]
    pos = np.arange(lw + tq)[None, :]
    dist = lw + i - pos
    count = np.zeros(dist.shape, np.int32)
    for w, d in BRANCHES:
        count += ((dist >= 0) & (dist <= w) & (dist % d == 0)).astype(np.int32)
    bias = _bucket_bias(bias_table, dist)
    logc = jnp.log(jnp.asarray(np.maximum(count, 1)).astype(bias_table.dtype)).astype(F32)
    full = jnp.where(jnp.asarray(count > 0), bias + logc, NEG_INF)
    return full[:, :, :lw], full[:, :, lw:]


def _sample_attn(q, kn, vn, kc_t, vc_t, bias_table):
    b, _, tq, _ = q.shape
    lw = kc_t.shape[3]
    new = pl.BlockSpec((1, H_ATTN, tq, HEAD_DIM), lambda bi: (bi, 0, 0, 0))
    cache = pl.BlockSpec((1, H_ATTN, HEAD_DIM, lw), lambda bi: (bi, 0, 0, 0))
    bias_c, bias_n = _sample_bias(bias_table, lw, tq)
    return pl.pallas_call(
        _sample_attn_kernel,
        grid=(b,),
        in_specs=[new, new, new, cache, cache, _full((H_ATTN, tq, lw)), _full((H_ATTN, tq, tq))],
        out_specs=new,
        out_shape=jax.ShapeDtypeStruct((b, H_ATTN, tq, HEAD_DIM), F32),
        compiler_params=_params(("parallel",)),
        name="sample_attn",
    )(q, kn, vn, kc_t, vc_t, bias_c, bias_n)


def _softplus(x):
    return jnp.maximum(x, 0.0) + jnp.log(1.0 + jnp.exp(-jnp.abs(x)))


def _rwkv_chunks(states, r, k, v, kk, a, cum, logw, last, c):
    tt = r.shape[0]
    nc = tt // c
    b = kk * a
    w_inv = jnp.exp(-cum)
    w_rem = jnp.exp(last - cum)
    rt = r * jnp.exp(cum)
    at = -kk * jnp.exp(cum - logw)
    bt = b * w_inv
    kt = k * w_inv
    bw = b * w_rem
    kw = k * w_rem
    w_end = jnp.exp(last)

    m0, m1 = _head_masks()
    lane2 = lax.broadcasted_iota(jnp.int32, (1, 2 * PAIR), 1)
    m0w = jnp.where(lane2 >= PAIR, lane2 - PAIR, lane2) < HEAD_DIM
    lane_c = lax.broadcasted_iota(jnp.int32, (c, 2 * c), 1)
    row_c = lax.broadcasted_iota(jnp.int32, (c, 2 * c), 0)
    left = lane_c < c
    col_c = jnp.where(left, lane_c, lane_c - c)
    strict = col_c < row_c
    incl = col_c <= row_c
    bd_c = ((lax.broadcasted_iota(jnp.int32, (2 * c, 2 * c), 0) >= c)
            == (lax.broadcasted_iota(jnp.int32, (2 * c, 2 * c), 1) >= c))
    bd_f = ((lax.broadcasted_iota(jnp.int32, (PAIR, PAIR), 0) >= HEAD_DIM)
            == (lax.broadcasted_iota(jnp.int32, (PAIR, PAIR), 1) >= HEAD_DIM))
    zeros = jnp.zeros((c, PAIR), BF16)

    items = [(n, pr) for n in range(nc) for pr in range(N_PAIRS)]
    cut = lambda x, it: x[it[0] * c:(it[0] + 1) * c, it[1] * PAIR:(it[1] + 1) * PAIR]
    sel = lambda m, x: jnp.where(m, x, jnp.zeros_like(x))
    cat0 = lambda *xs: jnp.concatenate(xs, axis=0)
    cat1 = lambda *xs: jnp.concatenate(xs, axis=1)

    def block_diag(x):
        xb = x.astype(BF16)
        return sel(bd_c, cat0(xb, xb))

    at_i = [cut(at, it).astype(BF16) for it in items]
    rt_i = [cut(rt, it) for it in items]
    v_i = [cut(v, it).astype(BF16) for it in items]
    bk = [cat0(cut(bt, it), cut(kt, it)).astype(BF16) for it in items]
    kb = [cat0(cut(kt, it), cut(bt, it)).astype(BF16) for it in items]
    rtb = [x.astype(BF16) for x in rt_i]
    aa0 = [_dot_nt(cat0(sel(m0, a_), sel(m0, r_)), y_) for a_, r_, y_ in zip(at_i, rtb, bk)]
    aa1 = [_dot_nt(cat0(sel(m1, a_), sel(m1, r_)), y_) for a_, r_, y_ in zip(at_i, rtb, kb)]
    x = [jnp.where(strict, jnp.where(left, p0[:c], p1[:c]), 0.0) for p0, p1 in zip(aa0, aa1)]
    akp = [jnp.where(strict, jnp.where(left, p1[:c], p0[:c]), 0.0) for p0, p1 in zip(aa0, aa1)]
    arb = [jnp.where(incl, jnp.where(left, p0[c:], p1[c:]), 0.0) for p0, p1 in zip(aa0, aa1)]
    ark = [jnp.where(incl, jnp.where(left, p1[c:], p0[c:]), 0.0) for p0, p1 in zip(aa0, aa1)]

    tp = x
    n = c
    while n > 2:
        xd = [block_diag(xi) for xi in x]
        x = [_dot(xi, d) for xi, d in zip(x, xd)]
        xd = [block_diag(xi) for xi in x]
        tp = [t + xi + _dot(t, d) for t, xi, d in zip(tp, x, xd)]
        n //= 2

    akv = [_dot(m_, cat0(sel(m1, v_), sel(m0, v_))) for m_, v_ in zip(akp, v_i)]
    rhs2 = [cat1(cut(at, it), q_) for it, q_ in zip(items, akv)]
    split = lambda x_: cat0(sel(m0w, x_), sel(~m0w, x_))
    pq = [x_ + _dot(t, split(x_.astype(BF16))) for x_, t in zip(rhs2, tp)]
    gy = [_dot(cat1(b_, k_), cat0(split(x_.astype(BF16)), cat1(zeros, sel(m1, v_)), cat1(zeros, sel(m0, v_))))
          for b_, k_, x_, v_ in zip(arb, ark, pq, v_i)]
    pg = [cat0(x_[:, :PAIR], r_ + g_[:, :PAIR]) for x_, r_, g_ in zip(pq, rt_i, gy)]

    states = list(states)
    y_rows = []
    for n in range(nc):
        idx = [n * N_PAIRS + pr for pr in range(N_PAIRS)]
        uy = [_dot_nt(pg[i], states[pr]) for pr, i in enumerate(idx)]
        u = [uy_[:c] + pq[i][:, PAIR:] for uy_, i in zip(uy, idx)]
        y_rows.append(cat1(*[uy_[c:] + gy[i][:, PAIR:] for uy_, i in zip(uy, idx)]))
        upd = [_dot_tn(cat0(u_.astype(BF16), v_i[i]), cat0(cut(bw, items[i]), cut(kw, items[i])))
               for u_, i in zip(u, idx)]
        states = [s * cut(w_end, items[i])[:1] + jnp.where(bd_f, d, 0.0) for s, d, i in zip(states, upd, idx)]
    return cat0(*y_rows), states


def _rwkv_kernel(p_ref, shift0_ref, s0_ref, mu_ref, w0_ref, wl_ref, a0_ref, al_ref, gl_ref,
                 kk_ref, ka_ref, rk_ref, lng_ref, lnb_ref, bd1_ref, bdm_ref,
                 y_ref, s_ref, carry, *, c):
    ci = pl.program_id(1)
    tt = p_ref.shape[1]

    @pl.when(ci == 0)
    def _():
        carry[...] = shift0_ref[0]
        s_ref[0] = s0_ref[0]

    p = p_ref[0]
    row = lax.broadcasted_iota(jnp.int32, (tt, 1), 0)
    prev = jnp.where(row == 0, carry[...], pltpu.roll(p, 1, 0))
    carry[...] = p[tt - 1:tt, :]
    xs = p + (prev - p) * mu_ref[...]
    r = xs[:, :B_W]
    k = xs[:, B_W:2 * B_W]
    v = xs[:, 2 * B_W:3 * B_W]
    xwa = xs[:, 3 * B_W:3 * B_W + LANES]
    xg = xs[:, 3 * B_W + LANES:]
    w_log = -_softplus(-(w0_ref[...] + _dot(jnp.tanh(xwa), wl_ref[...]))) - 0.5
    logw = -jnp.exp(w_log)
    a = jax.nn.sigmoid(a0_ref[...] + _dot(xwa, al_ref[...]))
    g = _dot(jax.nn.sigmoid(xg), gl_ref[...])
    bd1 = bd1_ref[...]
    kk = k * kk_ref[...]
    kk = kk / jnp.maximum(jnp.sqrt(_group_reduce(kk * kk, bd1)), 1e-12)
    k = k * (1.0 + (a - 1.0) * ka_ref[...])

    row_t = lax.broadcasted_iota(jnp.int32, (tt, tt), 0)
    col_t = lax.broadcasted_iota(jnp.int32, (tt, tt), 1)
    chunk_start = (row_t // c) * c
    tri = jnp.where((col_t <= row_t) & (col_t >= chunk_start), 1.0, 0.0).astype(BF16)
    hi, mid, lo = _split3(logw)
    cum = (jnp.dot(tri, hi, preferred_element_type=F32) + jnp.dot(tri, mid, preferred_element_type=F32)
           + jnp.dot(tri, lo, preferred_element_type=F32))
    cum3 = cum.reshape(tt // c, c, B_W)
    last = jnp.broadcast_to(cum3[:, c - 1:c, :], cum3.shape).reshape(tt, B_W)

    y, states = _rwkv_chunks([s_ref[0, pr] for pr in range(N_PAIRS)], r, k, v, kk, a, cum, logw, last, c)
    for pr in range(N_PAIRS):
        s_ref[0, pr] = states[pr]

    bdm = bdm_ref[...]
    dev = y - _group_reduce(y, bdm)
    yn = dev * lax.rsqrt(_group_reduce(dev * dev, bdm) + GN_EPS) * lng_ref[...] + lnb_ref[...]
    bonus = _group_reduce(r * k * rk_ref[...], bd1) * v
    y_ref[0] = ((yn + bonus) * g).astype(BF16)


def _rwkv(p, shift0, s0, weights, chunk, tile):
    b, t, _ = p.shape
    wspecs = [_full(w.shape) for w in weights]
    return pl.pallas_call(
        functools.partial(_rwkv_kernel, c=chunk),
        grid=(b, t // tile),
        in_specs=[pl.BlockSpec((1, tile, P_PAD), lambda bi, ci: (bi, ci, 0)),
                  pl.BlockSpec((1, 1, P_PAD), lambda bi, ci: (bi, 0, 0)),
                  pl.BlockSpec((1, N_PAIRS, PAIR, PAIR), lambda bi, ci: (bi, 0, 0, 0))] + wspecs,
        out_specs=[pl.BlockSpec((1, tile, B_W), lambda bi, ci: (bi, ci, 0)),
                   pl.BlockSpec((1, N_PAIRS, PAIR, PAIR), lambda bi, ci: (bi, 0, 0, 0))],
        out_shape=[jax.ShapeDtypeStruct((b, t, B_W), BF16),
                   jax.ShapeDtypeStruct((b, N_PAIRS, PAIR, PAIR), F32)],
        scratch_shapes=[pltpu.VMEM((1, P_PAD), F32)],
        compiler_params=_params(("parallel", "arbitrary")),
        name="rwkv7",
    )(p, shift0, s0, *weights)


def _pair_states(s):
    b = s.shape[0]
    s = s.reshape(b, N_PAIRS, 2, HEAD_DIM, HEAD_DIM)
    z = jnp.zeros_like(s[:, :, 0])
    top = jnp.concatenate([s[:, :, 0], z], axis=-1)
    bot = jnp.concatenate([z, s[:, :, 1]], axis=-1)
    return jnp.concatenate([top, bot], axis=-2)


def _unpair_states(s):
    return jnp.stack([s[:, :, :HEAD_DIM, :HEAD_DIM], s[:, :, HEAD_DIM:, HEAD_DIM:]], axis=2).reshape(
        s.shape[0], H_RWKV, HEAD_DIM, HEAD_DIM)


def _outmlp_kernel(*refs, dilations):
    n_branch = len(dilations)
    x_ref = refs[0]
    o_refs = refs[1:1 + n_branch]
    lse_refs = refs[1 + n_branch:1 + 2 * n_branch] if n_branch > 1 else ()
    n_in = 1 + n_branch + len(lse_refs)
    rw_ref, wo_ref, g2_ref, w1_ref, w2_ref, y_ref = refs[n_in:n_in + 6]
    stage = refs[n_in + 6:]
    tm = x_ref.shape[0]

    def natural(ref, d, st):
        if d == 1:
            return ref[...].astype(F32)
        for r in range(d):
            for j in range(A_W // LANES):
                lo = r * A_W + j * LANES
                st[j, pl.ds(r, tm // d, stride=d), :] = ref[:, lo:lo + LANES].astype(F32)
        return jnp.concatenate([st[j] for j in range(A_W // LANES)], axis=1)

    if n_branch > 1:
        lses = [natural(ref, d, stage[2 * i]) for i, (ref, d) in enumerate(zip(lse_refs, dilations))]
        top = functools.reduce(jnp.maximum, lses)
        num = den = None
        for i, (o_ref, d, lse) in enumerate(zip(o_refs, dilations, lses)):
            e = jnp.exp(lse - top)
            t = e * natural(o_ref, d, stage[2 * i + 1])
            num = t if num is None else num + t
            den = e if den is None else den + e
        att = num / den
    else:
        att = o_refs[0][...]
    h = x_ref[...] + _dot(att, wo_ref[:A_W, :]) + jnp.dot(rw_ref[...], wo_ref[A_W:, :],
                                                           preferred_element_type=F32)
    ms = jnp.mean(h * h, axis=-1, keepdims=True)
    m = (h * lax.rsqrt(ms + RMS_EPS) * g2_ref[...]).astype(BF16)
    acc = None
    for c in range(D_FF // D_MODEL):
        sl = slice(c * D_MODEL, (c + 1) * D_MODEL)
        u = jnp.maximum(jnp.dot(m, w1_ref[:, sl], preferred_element_type=F32), 0.0)
        t = jnp.dot((u * u).astype(BF16), w2_ref[sl, :], preferred_element_type=F32)
        acc = t if acc is None else acc + t
    y_ref[...] = h + acc


def _outmlp(x, os_, lses, dilations, rw, wo, g2, w1, w2, tm):
    n_tok = x.shape[0]
    tm = min(tm, n_tok)
    row = lambda w: pl.BlockSpec((tm, w), lambda i: (i, 0))
    view = lambda d: pl.BlockSpec((tm // d, d * A_W), lambda i: (i, 0))
    const = lambda shape: pl.BlockSpec(shape, lambda i: (0, 0), pipeline_mode=pl.Buffered(1))
    n_stage = 2 * len(dilations) if len(dilations) > 1 else 0
    return pl.pallas_call(
        functools.partial(_outmlp_kernel, dilations=dilations),
        grid=(n_tok // tm,),
        in_specs=[row(D_MODEL)] + [view(d) for d in dilations] + [view(d) for d in dilations[:len(lses)]]
        + [row(B_W), const((D_MODEL, D_MODEL)), const((1, D_MODEL)), const((D_MODEL, D_FF)),
           const((D_FF, D_MODEL))],
        out_specs=row(D_MODEL),
        out_shape=jax.ShapeDtypeStruct((n_tok, D_MODEL), F32),
        scratch_shapes=[pltpu.VMEM((A_W // LANES, tm, LANES), F32)] * n_stage,
        compiler_params=_params(("parallel",)),
        name="outmlp",
    )(x, *os_, *lses, rw, wo, g2, w1, w2)


def _row(v, width=None):
    v = v.reshape(1, -1).astype(F32)
    if width is not None and v.shape[1] < width:
        v = jnp.pad(v, ((0, 0), (0, width - v.shape[1])))
    return v


def _tile_heads(g, n_heads):
    return jnp.tile(g.reshape(1, HEAD_DIM).astype(F32), (1, n_heads))


def kernel(x_prompt, x_sample, cache_k_win, cache_v_win, state_wkv, state_shift, bias_table, ln1_g, w_in,
           q_norm_g, k_norm_g, mu_shift, w0, w_lora2, a0, a_lora2, g_lora2, k_k, k_a, r_k, lnx_g, lnx_b,
           w_out, ln2_g, w_mlp1, w_mlp2):
    depth = w_in.shape[0]
    assert depth == 1, "a deeper stack would repeat the per-layer calls below"
    li = 0
    bp, tp, _ = x_prompt.shape
    bs, ts, _ = x_sample.shape
    keep = min(MAX_WINDOW, tp)

    wqkv = w_in[li][:, :3 * A_W].astype(BF16)
    wp = jnp.pad(w_in[li][:, 3 * A_W:], ((0, 0), (0, P_PAD - N_SHIFT))).astype(BF16)
    gq = _tile_heads(q_norm_g[li], H_ATTN)
    gk = _tile_heads(k_norm_g[li], H_ATTN)
    zrow = lambda n: jnp.zeros((n, B_W), F32)
    rwkv_weights = [
        _row(mu_shift[li], P_PAD), _row(w0[li]),
        jnp.concatenate([w_lora2[li].astype(F32), zrow(LANES - LORA_W)], axis=0).astype(BF16),
        _row(a0[li]),
        jnp.concatenate([zrow(LORA_W), a_lora2[li].astype(F32)], axis=0).astype(BF16),
        jnp.concatenate([g_lora2[li].astype(F32), zrow(LORA_PAD - LORA_G)], axis=0).astype(BF16),
        _row(k_k[li]), _row(k_a[li]), _row(r_k[li]), _row(lnx_g[li]), _row(lnx_b[li]),
        _block_diag(B_W, 1.0), _block_diag(B_W, 1.0 / HEAD_DIM),
    ]
    wo = w_out[li].astype(BF16)
    w1 = w_mlp1[li].astype(BF16)
    w2 = w_mlp2[li].astype(BF16)
    g1 = _row(ln1_g[li])
    g2 = _row(ln2_g[li])

    xp = x_prompt.reshape(bp * tp, D_MODEL)
    dils = tuple(d for _, d in BRANCHES)
    q, k, v, kf, vf, p, *views = _inproj(xp, g1, wqkv, wp, gq, gk, tm=512, seq=tp, keep=keep,
                                         dilations=dils[1:])
    views = [q, k, v] + views
    os_, lses = [], []
    for i, d in enumerate(dils):
        qd, kd, vd = (a.reshape(bp, tp // d, d * A_W) for a in views[3 * i:3 * i + 3])
        o, lse = _attn_branch(qd, kd, vd, _branch_bias(bias_table, d), d, tq=1024)
        os_.append(o.reshape(bp * tp // d, d * A_W))
        lses.append(lse.reshape(bp * tp // d, d * A_W))
    p3 = p.reshape(bp, tp, P_PAD)
    rw, s_p = _rwkv(p3, jnp.zeros((bp, 1, P_PAD), F32), jnp.zeros((bp, N_PAIRS, PAIR, PAIR), F32),
                    rwkv_weights, chunk=64, tile=256)
    y_p = _outmlp(xp, os_, lses, dils, rw.reshape(bp * tp, B_W), wo, g2, w1, w2, tm=512)
    heads = lambda a, b_, t_: a.reshape(b_, t_, H_ATTN, HEAD_DIM)
    k_win_p = heads(kf, bp, keep)[None]
    v_win_p = heads(vf, bp, keep)[None]
    wkv_p = _unpair_states(s_p)[None]
    shift_p = p3[:, -1, :N_SHIFT][None]

    xs = x_sample.reshape(bs * ts, D_MODEL)
    qs, _, _, kfs, vfs, ps = _inproj(xs, g1, wqkv, wp, gq, gk, tm=512)
    head_major = lambda a: a.reshape(bs, ts, H_ATTN, HEAD_DIM).transpose(0, 2, 1, 3)
    cache_t = lambda c: c[li].transpose(0, 2, 3, 1)
    att_s = _sample_attn(head_major(qs.astype(F32)), head_major(kfs), head_major(vfs),
                         cache_t(cache_k_win), cache_t(cache_v_win), bias_table)
    att_s = att_s.transpose(0, 2, 1, 3)
    ps3 = ps.reshape(bs, ts, P_PAD)
    shift0 = jnp.pad(state_shift[li].astype(F32), ((0, 0), (0, P_PAD - N_SHIFT))).reshape(bs, 1, P_PAD)
    rw_s, s_s = _rwkv(ps3, shift0, _pair_states(state_wkv[li].astype(F32)), rwkv_weights, chunk=ts, tile=ts)
    y_s = _outmlp(xs, [att_s.reshape(bs * ts, A_W)], [], (1,), rw_s.reshape(bs * ts, B_W), wo, g2, w1, w2, tm=256)

    return (y_p.reshape(bp, tp, D_MODEL), y_s.reshape(bs, ts, D_MODEL), k_win_p, v_win_p, wkv_p, shift_p,
            heads(kfs, bs, ts)[None], heads(vfs, bs, ts)[None], _unpair_states(s_s)[None],
            ps3[:, -1, :N_SHIFT][None])
```

```python
import functools
import math

import numpy as np
import jax
import jax.numpy as jnp
from jax import lax
from jax.experimental import pallas as pl
from jax.experimental.pallas import tpu as pltpu

F32 = jnp.float32
BF16 = jnp.bfloat16

D_MODEL = 1024
HEAD_DIM = 64
A_W = 512
B_W = 512
H_ATTN = A_W // HEAD_DIM
H_RWKV = B_W // HEAD_DIM
BRANCHES = ((128, 1), (512, 4), (2048, 16))
MAX_WINDOW = 2048
WIN = 128
NUM_BUCKETS = 32
MAX_DISTANCE = 2048
LORA_W = 64
LORA_A = 64
LORA_G = 160
N_SHIFT = 3 * B_W + LORA_W + LORA_A + LORA_G
D_FF = 4 * D_MODEL
RMS_EPS = 1e-6
GN_EPS = 64e-5
NEG_INF = -1e30

LANES = 128
PAIR = 2 * HEAD_DIM
N_PAIRS = B_W // PAIR
P_PAD = 1920
LORA_PAD = P_PAD - 3 * B_W - LANES
VMEM_LIMIT = 56 * 1024 * 1024
ATTN_BLOCKS_PER_STEP = 4


def _dot(a, b):
    return jnp.dot(a.astype(BF16), b.astype(BF16), preferred_element_type=F32)


def _dot_nt(a, b):
    return lax.dot_general(a.astype(BF16), b.astype(BF16), (((1,), (1,)), ((), ())),
                           preferred_element_type=F32)


def _dot_tn(a, b):
    return lax.dot_general(a.astype(BF16), b.astype(BF16), (((0,), (0,)), ((), ())),
                           preferred_element_type=F32)


def _split2(x):
    hi = x.astype(BF16)
    lo = (x - hi.astype(F32)).astype(BF16)
    return hi, lo


def _split3(x):
    hi = x.astype(BF16)
    r1 = x - hi.astype(F32)
    mid = r1.astype(BF16)
    lo = (r1 - mid.astype(F32)).astype(BF16)
    return hi, mid, lo


def _group_reduce(x, bd, split=True):
    hi, lo = _split2(x)
    out = jnp.dot(hi, bd, preferred_element_type=F32)
    return out + jnp.dot(lo, bd, preferred_element_type=F32) if split else out


def _block_diag(n, value):
    idx = np.arange(n) // HEAD_DIM
    return jnp.asarray(np.where(idx[:, None] == idx[None, :], value, 0.0), BF16)


def _full(shape):
    return pl.BlockSpec(shape, lambda *_: (0,) * len(shape))


def _params(sem):
    return pltpu.CompilerParams(dimension_semantics=sem, vmem_limit_bytes=VMEM_LIMIT)


def _inproj_kernel(x_ref, g1_ref, wqkv_ref, wp_ref, gq_ref, gk_ref, bd_ref, *refs, dilations):
    n_view = 3 * len(dilations)
    q_ref, k_ref, v_ref, kf_ref, vf_ref, p_ref = refs[:6]
    view_refs = refs[6:6 + n_view]
    stage = refs[6 + n_view:]
    x = x_ref[...]
    ms = jnp.mean(x * x, axis=-1, keepdims=True)
    n = (x * lax.rsqrt(ms + RMS_EPS) * g1_ref[...]).astype(BF16)
    p_ref[...] = jnp.dot(n, wp_ref[...], preferred_element_type=F32)
    qkv = jnp.dot(n, wqkv_ref[...], preferred_element_type=F32)
    q = qkv[:, :A_W]
    k = qkv[:, A_W:2 * A_W]
    v = qkv[:, 2 * A_W:]
    bd = bd_ref[...]
    qn = q * lax.rsqrt(_group_reduce(q * q, bd, split=False) + RMS_EPS) * gq_ref[...]
    kn = k * lax.rsqrt(_group_reduce(k * k, bd, split=False) + RMS_EPS) * gk_ref[...]
    qs = qn * (HEAD_DIM ** -0.5)
    q_ref[...] = qs.astype(BF16)
    k_ref[...] = kn.astype(BF16)
    v_ref[...] = v.astype(BF16)
    kf_ref[...] = kn
    vf_ref[...] = v
    if dilations:
        tm = x.shape[0]
        for a, st in zip((qs, kn, v), stage):
            for j in range(A_W // LANES):
                st[j] = a[:, j * LANES:(j + 1) * LANES]
        for di, d in enumerate(dilations):
            for ai, st in enumerate(stage):
                out = view_refs[3 * di + ai]
                for r in range(d):
                    for j in range(A_W // LANES):
                        lo = r * A_W + j * LANES
                        out[:, lo:lo + LANES] = st[j, pl.ds(r, tm // d, stride=d), :].astype(BF16)


def _inproj(x, g1, wqkv, wp, gq, gk, tm, seq=None, keep=None, dilations=()):
    n_tok = x.shape[0]
    tm = min(tm, n_tok)
    seq = n_tok if seq is None else seq
    keep = seq if keep is None else keep
    tiles, kept, skip = seq // tm, keep // tm, (seq - keep) // tm
    row = lambda w: pl.BlockSpec((tm, w), lambda i: (i, 0))
    tail = pl.BlockSpec((tm, A_W), lambda i: ((i // tiles) * kept + jnp.maximum(i % tiles - skip, 0), 0))
    view_specs = [pl.BlockSpec((tm // d, d * A_W), lambda i: (i, 0)) for d in dilations for _ in range(3)]
    view_shapes = [jax.ShapeDtypeStruct((n_tok // d, d * A_W), BF16) for d in dilations for _ in range(3)]
    return pl.pallas_call(
        functools.partial(_inproj_kernel, dilations=dilations),
        grid=(n_tok // tm,),
        in_specs=[row(D_MODEL), _full((1, D_MODEL)), _full((D_MODEL, 3 * A_W)), _full((D_MODEL, P_PAD)),
                  _full((1, A_W)), _full((1, A_W)), _full((A_W, A_W))],
        out_specs=[row(A_W), row(A_W), row(A_W), tail, tail, row(P_PAD)] + view_specs,
        out_shape=[jax.ShapeDtypeStruct((n_tok, A_W), BF16)] * 3
        + [jax.ShapeDtypeStruct((n_tok // seq * keep, A_W), F32)] * 2
        + [jax.ShapeDtypeStruct((n_tok, P_PAD), F32)] + view_shapes,
        scratch_shapes=[pltpu.VMEM((A_W // LANES, tm, LANES), F32)] * (3 if dilations else 0),
        compiler_params=_params(("arbitrary",)),
        name="inproj",
    )(x, g1, wqkv, wp, gq, gk, _block_diag(A_W, 1.0 / HEAD_DIM))


def _head_masks():
    lane = lax.broadcasted_iota(jnp.int32, (1, PAIR), 1)
    return lane < HEAD_DIM, lane >= HEAD_DIM


def _attn_kernel(q_ref, kh_ref, kc_ref, vh_ref, vc_ref, bias_ref, o_ref, lse_ref, kbuf, vbuf, *, tq):
    i = pl.program_id(2)
    kbuf[0:WIN, :] = kh_ref[0]
    kbuf[WIN:, :] = kc_ref[0]
    vbuf[0:WIN, :] = vh_ref[0]
    vbuf[WIN:, :] = vc_ref[0]
    masks = _head_masks()

    blocks = min(ATTN_BLOCKS_PER_STEP, tq // WIN)

    def body(j, carry):
        offs = [pl.multiple_of((j * blocks + u) * WIN, WIN) for u in range(blocks)]
        first = jnp.logical_and(i == 0, j == 0).astype(jnp.int32)
        pair = lambda h: slice((h // 2) * PAIR, (h // 2 + 1) * PAIR)

        def scores(u, h):
            q2 = q_ref[0, pl.ds(offs[u], WIN), pair(h)]
            qm = jnp.where(masks[h % 2], q2, jnp.zeros_like(q2))
            return _dot_nt(qm, kbuf[pl.ds(offs[u], 2 * WIN), pair(h)]) + bias_ref[first if u == 0 else 0, h]

        s_all = [[scores(u, h) for h in range(H_ATTN)] for u in range(blocks)]
        ones = [jnp.broadcast_to(jnp.where(mk, 1.0, 0.0).astype(BF16), (2 * WIN, PAIR)) for mk in masks]
        for u in range(blocks):
            for pr in range(N_PAIRS):
                es, ms = [], []
                for h in (2 * pr, 2 * pr + 1):
                    s = s_all[u][h]
                    m = jnp.max(s, axis=-1, keepdims=True)
                    es.append(jnp.exp(s - m).astype(BF16))
                    ms.append(m)
                v2 = vbuf[pl.ds(offs[u], 2 * WIN), pair(2 * pr)]
                rhs = jnp.concatenate(
                    [jnp.concatenate([jnp.where(mk, v2, jnp.zeros_like(v2)), on], axis=1)
                     for mk, on in zip(masks, ones)], axis=0)
                res = jnp.dot(jnp.concatenate(es, axis=1), rhs, preferred_element_type=F32)
                l_pair = res[:, PAIR:]
                o_ref[0, pl.ds(offs[u], WIN), pair(2 * pr)] = (res[:, :PAIR] / l_pair).astype(BF16)
                lse_ref[0, pl.ds(offs[u], WIN), pair(2 * pr)] = (jnp.where(masks[0], ms[0], ms[1])
                                                                 + jnp.log(l_pair))
        return carry

    lax.fori_loop(0, tq // (WIN * blocks), body, 0)


def _attn_branch(q, k, v, bias, d, tq):
    b, l, _ = q.shape
    tq = min(tq, l)
    sub = tq // WIN
    cur = pl.BlockSpec((1, tq, A_W), lambda bi, r, i: (bi, i, r))
    halo = pl.BlockSpec((1, WIN, A_W), lambda bi, r, i: (bi, jnp.maximum(i * sub - 1, 0), r))
    return pl.pallas_call(
        functools.partial(_attn_kernel, tq=tq),
        grid=(b, d, l // tq),
        in_specs=[cur, halo, cur, halo, cur, _full((2, H_ATTN, WIN, 2 * WIN))],
        out_specs=[cur, cur],
        out_shape=[jax.ShapeDtypeStruct((b, l, d * A_W), BF16), jax.ShapeDtypeStruct((b, l, d * A_W), F32)],
        scratch_shapes=[pltpu.VMEM((tq + WIN, A_W), BF16), pltpu.VMEM((tq + WIN, A_W), BF16)],
        compiler_params=_params(("parallel", "parallel", "arbitrary")),
        name=f"attn_d{d}",
    )(q, k, k, v, v, bias)


def _t5_bucket(dist):
    dist = np.maximum(dist, 0)
    max_exact = NUM_BUCKETS // 2
    large = max_exact + (np.log(np.maximum(dist, 1) / max_exact)
                         / math.log(MAX_DISTANCE / max_exact)
                         * (NUM_BUCKETS - max_exact)).astype(np.int32)
    large = np.minimum(large, NUM_BUCKETS - 1)
    return np.where(dist < max_exact, dist, large).astype(np.int32)


def _bucket_bias(bias_table, dist):
    onehot = (_t5_bucket(dist).reshape(-1, 1) == np.arange(NUM_BUCKETS)[None, :])
    rows = jnp.dot(jnp.asarray(onehot, bias_table.dtype), bias_table, precision=lax.Precision.HIGHEST)
    return rows.T.reshape((bias_table.shape[1],) + dist.shape).astype(F32)


def _branch_bias(bias_table, d):
    m = np.arange(WIN)[:, None]
    n = np.arange(2 * WIN)[None, :]
    steps = m + WIN - n
    valid = (steps >= 0) & (steps <= WIN)
    bias = _bucket_bias(bias_table, steps * d)
    normal = jnp.where(jnp.asarray(valid), bias, NEG_INF)
    first = jnp.where(jnp.asarray(valid & (n >= WIN)), bias, NEG_INF)
    return jnp.stack([normal, first])


def _sample_attn_kernel(q_ref, kn_ref, vn_ref, kc_ref, vc_ref, bias_ref, biasn_ref, o_ref):
    heads = range(H_ATTN)
    q = [q_ref[0, h].astype(BF16) for h in heads]
    s_c = [jnp.dot(q[h], kc_ref[0, h].astype(BF16), preferred_element_type=F32) + bias_ref[h] for h in heads]
    s_n = [_dot_nt(q[h], kn_ref[0, h]) + biasn_ref[h] for h in heads]
    for h in heads:
        m = jnp.maximum(jnp.max(s_c[h], axis=-1, keepdims=True), jnp.max(s_n[h], axis=-1, keepdims=True))
        e_c = jnp.exp(s_c[h] - m)
        e_n = jnp.exp(s_n[h] - m)
        l = jnp.sum(e_c, axis=-1, keepdims=True) + jnp.sum(e_n, axis=-1, keepdims=True)
        o = _dot_nt(e_c, vc_ref[0, h]) + _dot(e_n, vn_ref[0, h])
        o_ref[0, h] = o / l


def _sample_bias(bias_table, lw, tq):
    i = np.arange(tq)[:, None]
    pos = np.arange(lw + tq)[None, :]
    dist = lw + i - pos
    count = np.zeros(dist.shape, np.int32)
    for w, d in BRANCHES:
        count += ((dist >= 0) & (dist <= w) & (dist % d == 0)).astype(np.int32)
    bias = _bucket_bias(bias_table, dist)
    logc = jnp.log(jnp.asarray(np.maximum(count, 1)).astype(bias_table.dtype)).astype(F32)
    full = jnp.where(jnp.asarray(count > 0), bias + logc, NEG_INF)
    return full[:, :, :lw], full[:, :, lw:]


def _sample_attn(q, kn, vn, kc_t, vc_t, bias_table):
    b, _, tq, _ = q.shape
    lw = kc_t.shape[3]
    new = pl.BlockSpec((1, H_ATTN, tq, HEAD_DIM), lambda bi: (bi, 0, 0, 0))
    cache = pl.BlockSpec((1, H_ATTN, HEAD_DIM, lw), lambda bi: (bi, 0, 0, 0))
    bias_c, bias_n = _sample_bias(bias_table, lw, tq)
    return pl.pallas_call(
        _sample_attn_kernel,
        grid=(b,),
        in_specs=[new, new, new, cache, cache, _full((H_ATTN, tq, lw)), _full((H_ATTN, tq, tq))],
        out_specs=new,
        out_shape=jax.ShapeDtypeStruct((b, H_ATTN, tq, HEAD_DIM), F32),
        compiler_params=_params(("parallel",)),
        name="sample_attn",
    )(q, kn, vn, kc_t, vc_t, bias_c, bias_n)


def _softplus(x):
    return jnp.maximum(x, 0.0) + jnp.log(1.0 + jnp.exp(-jnp.abs(x)))


def _rwkv_chunks(states, r, k, v, kk, a, cum, logw, last, c, seqs):
    tt = r.shape[0]
    nc = tt // c
    b = kk * a
    w_inv = jnp.exp(-cum)
    w_rem = jnp.exp(last - cum)
    rt = r * jnp.exp(cum)
    at = -kk * jnp.exp(cum - logw)
    bt = b * w_inv
    kt = k * w_inv
    bw = b * w_rem
    kw = k * w_rem
    w_end = jnp.exp(last)

    m0, m1 = _head_masks()
    lane2 = lax.broadcasted_iota(jnp.int32, (1, 2 * PAIR), 1)
    m0w = jnp.where(lane2 >= PAIR, lane2 - PAIR, lane2) < HEAD_DIM
    lane_c = lax.broadcasted_iota(jnp.int32, (c, 2 * c), 1)
    row_c = lax.broadcasted_iota(jnp.int32, (c, 2 * c), 0)
    left = lane_c < c
    col_c = jnp.where(left, lane_c, lane_c - c)
    strict = col_c < row_c
    incl = col_c <= row_c
    bd_c = ((lax.broadcasted_iota(jnp.int32, (2 * c, 2 * c), 0) >= c)
            == (lax.broadcasted_iota(jnp.int32, (2 * c, 2 * c), 1) >= c))
    bd_f = ((lax.broadcasted_iota(jnp.int32, (PAIR, PAIR), 0) >= HEAD_DIM)
            == (lax.broadcasted_iota(jnp.int32, (PAIR, PAIR), 1) >= HEAD_DIM))
    zeros = jnp.zeros((c, PAIR), BF16)

    items = [(n, pr) for n in range(nc) for pr in range(N_PAIRS)]
    cut = lambda x, it: x[it[0] * c:(it[0] + 1) * c, it[1] * PAIR:(it[1] + 1) * PAIR]
    sel = lambda m, x: jnp.where(m, x, jnp.zeros_like(x))
    cat0 = lambda *xs: jnp.concatenate(xs, axis=0)
    cat1 = lambda *xs: jnp.concatenate(xs, axis=1)

    def block_diag(x):
        xb = x.astype(BF16)
        return sel(bd_c, cat0(xb, xb))

    at_i = [cut(at, it).astype(BF16) for it in items]
    rt_i = [cut(rt, it) for it in items]
    v_i = [cut(v, it).astype(BF16) for it in items]
    bk = [cat0(cut(bt, it), cut(kt, it)).astype(BF16) for it in items]
    kb = [cat0(cut(kt, it), cut(bt, it)).astype(BF16) for it in items]
    rtb = [x.astype(BF16) for x in rt_i]
    aa0 = [_dot_nt(cat0(sel(m0, a_), sel(m0, r_)), y_) for a_, r_, y_ in zip(at_i, rtb, bk)]
    aa1 = [_dot_nt(cat0(sel(m1, a_), sel(m1, r_)), y_) for a_, r_, y_ in zip(at_i, rtb, kb)]
    x = [jnp.where(strict, jnp.where(left, p0[:c], p1[:c]), 0.0) for p0, p1 in zip(aa0, aa1)]
    akp = [jnp.where(strict, jnp.where(left, p1[:c], p0[:c]), 0.0) for p0, p1 in zip(aa0, aa1)]
    arb = [jnp.where(incl, jnp.where(left, p0[c:], p1[c:]), 0.0) for p0, p1 in zip(aa0, aa1)]
    ark = [jnp.where(incl, jnp.where(left, p1[c:], p0[c:]), 0.0) for p0, p1 in zip(aa0, aa1)]

    tp = x
    n = c
    while n > 2:
        xd = [block_diag(xi) for xi in x]
        x = [_dot(xi, d) for xi, d in zip(x, xd)]
        xd = [block_diag(xi) for xi in x]
        tp = [t + xi + _dot(t, d) for t, xi, d in zip(tp, x, xd)]
        n //= 2

    akv = [_dot(m_, cat0(sel(m1, v_), sel(m0, v_))) for m_, v_ in zip(akp, v_i)]
    rhs2 = [cat1(cut(at, it), q_) for it, q_ in zip(items, akv)]
    split = lambda x_: cat0(sel(m0w, x_), sel(~m0w, x_))
    pq = [x_ + _dot(t, split(x_.astype(BF16))) for x_, t in zip(rhs2, tp)]
    gy = [_dot(cat1(b_, k_), cat0(split(x_.astype(BF16)), cat1(zeros, sel(m1, v_)), cat1(zeros, sel(m0, v_))))
          for b_, k_, x_, v_ in zip(arb, ark, pq, v_i)]
    pg = [cat0(x_[:, :PAIR], r_ + g_[:, :PAIR]) for x_, r_, g_ in zip(pq, rt_i, gy)]

    states = list(states)
    per_seq = nc // seqs
    y_rows = [None] * nc
    for j in range(per_seq):
        idx = [(sq * per_seq + j) * N_PAIRS + pr for sq in range(seqs) for pr in range(N_PAIRS)]
        uy = [_dot_nt(pg[i], s) for s, i in zip(states, idx)]
        u = [uy_[:c] + pq[i][:, PAIR:] for uy_, i in zip(uy, idx)]
        ys = [uy_[c:] + gy[i][:, PAIR:] for uy_, i in zip(uy, idx)]
        for sq in range(seqs):
            y_rows[sq * per_seq + j] = cat1(*ys[sq * N_PAIRS:(sq + 1) * N_PAIRS])
        upd = [_dot_tn(cat0(u_.astype(BF16), v_i[i]), cat0(cut(bw, items[i]), cut(kw, items[i])))
               for u_, i in zip(u, idx)]
        states = [s * cut(w_end, items[i])[:1] + jnp.where(bd_f, d, 0.0) for s, d, i in zip(states, upd, idx)]
    return cat0(*y_rows), states


def _rwkv_kernel(p_ref, shift0_ref, s0_ref, mu_ref, w0_ref, wl_ref, a0_ref, al_ref, gl_ref,
                 kk_ref, ka_ref, rk_ref, lng_ref, lnb_ref, bd1_ref, bdm_ref,
                 y_ref, s_ref, carry, *, c):
    ci = pl.program_id(1)
    seqs, tile = p_ref.shape[0], p_ref.shape[1]
    tt = seqs * tile

    @pl.when(ci == 0)
    def _():
        carry[...] = shift0_ref[...]
        s_ref[...] = s0_ref[...]

    p = jnp.concatenate([p_ref[sq] for sq in range(seqs)], axis=0)
    row = lax.broadcasted_iota(jnp.int32, (tt, 1), 0)
    prev = pltpu.roll(p, 1, 0)
    for sq in range(seqs):
        prev = jnp.where(row == sq * tile, carry[sq], prev)
        carry[sq] = p[(sq + 1) * tile - 1:(sq + 1) * tile, :]
    xs = p + (prev - p) * mu_ref[...]
    r = xs[:, :B_W]
    k = xs[:, B_W:2 * B_W]
    v = xs[:, 2 * B_W:3 * B_W]
    xwa = xs[:, 3 * B_W:3 * B_W + LANES]
    xg = xs[:, 3 * B_W + LANES:]
    w_log = -_softplus(-(w0_ref[...] + _dot(jnp.tanh(xwa), wl_ref[...]))) - 0.5
    logw = -jnp.exp(w_log)
    a = jax.nn.sigmoid(a0_ref[...] + _dot(xwa, al_ref[...]))
    g = _dot(jax.nn.sigmoid(xg), gl_ref[...])
    bd1 = bd1_ref[...]
    kk = k * kk_ref[...]
    kk = kk / jnp.maximum(jnp.sqrt(_group_reduce(kk * kk, bd1)), 1e-12)
    k = k * (1.0 + (a - 1.0) * ka_ref[...])

    grp = tile if tile >= LANES else tt
    row_t = lax.broadcasted_iota(jnp.int32, (grp, grp), 0)
    col_t = lax.broadcasted_iota(jnp.int32, (grp, grp), 1)
    tri = jnp.where((col_t <= row_t) & (col_t >= (row_t // c) * c), 1.0, 0.0).astype(BF16)
    parts = _split3(logw)
    cum = jnp.concatenate(
        [sum(jnp.dot(tri, x[i * grp:(i + 1) * grp], preferred_element_type=F32) for x in parts)
         for i in range(tt // grp)], axis=0)
    cum3 = cum.reshape(tt // c, c, B_W)
    last = jnp.broadcast_to(cum3[:, c - 1:c, :], cum3.shape).reshape(tt, B_W)

    y, states = _rwkv_chunks([s_ref[sq, pr] for sq in range(seqs) for pr in range(N_PAIRS)],
                             r, k, v, kk, a, cum, logw, last, c, seqs)
    for sq in range(seqs):
        for pr in range(N_PAIRS):
            s_ref[sq, pr] = states[sq * N_PAIRS + pr]

    bdm = bdm_ref[...]
    dev = y - _group_reduce(y, bdm)
    yn = dev * lax.rsqrt(_group_reduce(dev * dev, bdm) + GN_EPS) * lng_ref[...] + lnb_ref[...]
    bonus = _group_reduce(r * k * rk_ref[...], bd1) * v
    out = (yn + bonus) * g
    for sq in range(seqs):
        y_ref[sq] = out[sq * tile:(sq + 1) * tile].astype(BF16)


def _rwkv(p, shift0, s0, weights, chunk, tile, seqs):
    b, t, _ = p.shape
    seqs = math.gcd(seqs, b)
    wspecs = [_full(w.shape) for w in weights]
    return pl.pallas_call(
        functools.partial(_rwkv_kernel, c=chunk),
        grid=(b // seqs, t // tile),
        in_specs=[pl.BlockSpec((seqs, tile, P_PAD), lambda bi, ci: (bi, ci, 0)),
                  pl.BlockSpec((seqs, 1, P_PAD), lambda bi, ci: (bi, 0, 0)),
                  pl.BlockSpec((seqs, N_PAIRS, PAIR, PAIR), lambda bi, ci: (bi, 0, 0, 0))] + wspecs,
        out_specs=[pl.BlockSpec((seqs, tile, B_W), lambda bi, ci: (bi, ci, 0)),
                   pl.BlockSpec((seqs, N_PAIRS, PAIR, PAIR), lambda bi, ci: (bi, 0, 0, 0))],
        out_shape=[jax.ShapeDtypeStruct((b, t, B_W), BF16),
                   jax.ShapeDtypeStruct((b, N_PAIRS, PAIR, PAIR), F32)],
        scratch_shapes=[pltpu.VMEM((seqs, 1, P_PAD), F32)],
        compiler_params=_params(("parallel", "arbitrary")),
        name="rwkv7",
    )(p, shift0, s0, *weights)


def _pair_states(s):
    b = s.shape[0]
    s = s.reshape(b, N_PAIRS, 2, HEAD_DIM, HEAD_DIM)
    z = jnp.zeros_like(s[:, :, 0])
    top = jnp.concatenate([s[:, :, 0], z], axis=-1)
    bot = jnp.concatenate([z, s[:, :, 1]], axis=-1)
    return jnp.concatenate([top, bot], axis=-2)


def _unpair_states(s):
    return jnp.stack([s[:, :, :HEAD_DIM, :HEAD_DIM], s[:, :, HEAD_DIM:, HEAD_DIM:]], axis=2).reshape(
        s.shape[0], H_RWKV, HEAD_DIM, HEAD_DIM)


def _outmlp_kernel(*refs, dilations):
    n_branch = len(dilations)
    x_ref = refs[0]
    o_refs = refs[1:1 + n_branch]
    lse_refs = refs[1 + n_branch:1 + 2 * n_branch] if n_branch > 1 else ()
    n_in = 1 + n_branch + len(lse_refs)
    rw_ref, wo_ref, g2_ref, w1_ref, w2_ref, y_ref = refs[n_in:n_in + 6]
    stage = refs[n_in + 6:]
    tm = x_ref.shape[0]

    def natural(ref, d, st):
        if d == 1:
            return ref[...].astype(F32)
        for r in range(d):
            for j in range(A_W // LANES):
                lo = r * A_W + j * LANES
                st[j, pl.ds(r, tm // d, stride=d), :] = ref[:, lo:lo + LANES].astype(F32)
        return jnp.concatenate([st[j] for j in range(A_W // LANES)], axis=1)

    if n_branch > 1:
        lses = [natural(ref, d, stage[2 * i]) for i, (ref, d) in enumerate(zip(lse_refs, dilations))]
        top = functools.reduce(jnp.maximum, lses)
        num = den = None
        for i, (o_ref, d, lse) in enumerate(zip(o_refs, dilations, lses)):
            e = jnp.exp(lse - top)
            t = e * natural(o_ref, d, stage[2 * i + 1])
            num = t if num is None else num + t
            den = e if den is None else den + e
        att = num / den
    else:
        att = o_refs[0][...]
    h = x_ref[...] + _dot(att, wo_ref[:A_W, :]) + jnp.dot(rw_ref[...], wo_ref[A_W:, :],
                                                           preferred_element_type=F32)
    ms = jnp.mean(h * h, axis=-1, keepdims=True)
    m = (h * lax.rsqrt(ms + RMS_EPS) * g2_ref[...]).astype(BF16)
    acc = None
    for c in range(D_FF // D_MODEL):
        sl = slice(c * D_MODEL, (c + 1) * D_MODEL)
        u = jnp.maximum(jnp.dot(m, w1_ref[:, sl], preferred_element_type=F32), 0.0)
        t = jnp.dot((u * u).astype(BF16), w2_ref[sl, :], preferred_element_type=F32)
        acc = t if acc is None else acc + t
    y_ref[...] = h + acc


def _outmlp(x, os_, lses, dilations, rw, wo, g2, w1, w2, tm):
    n_tok = x.shape[0]
    tm = min(tm, n_tok)
    row = lambda w: pl.BlockSpec((tm, w), lambda i: (i, 0))
    view = lambda d: pl.BlockSpec((tm // d, d * A_W), lambda i: (i, 0))
    const = lambda shape: pl.BlockSpec(shape, lambda i: (0, 0), pipeline_mode=pl.Buffered(1))
    n_stage = 2 * len(dilations) if len(dilations) > 1 else 0
    return pl.pallas_call(
        functools.partial(_outmlp_kernel, dilations=dilations),
        grid=(n_tok // tm,),
        in_specs=[row(D_MODEL)] + [view(d) for d in dilations] + [view(d) for d in dilations[:len(lses)]]
        + [row(B_W), const((D_MODEL, D_MODEL)), const((1, D_MODEL)), const((D_MODEL, D_FF)),
           const((D_FF, D_MODEL))],
        out_specs=row(D_MODEL),
        out_shape=jax.ShapeDtypeStruct((n_tok, D_MODEL), F32),
        scratch_shapes=[pltpu.VMEM((A_W // LANES, tm, LANES), F32)] * n_stage,
        compiler_params=_params(("parallel",)),
        name="outmlp",
    )(x, *os_, *lses, rw, wo, g2, w1, w2)


def _row(v, width=None):
    v = v.reshape(1, -1).astype(F32)
    if width is not None and v.shape[1] < width:
        v = jnp.pad(v, ((0, 0), (0, width - v.shape[1])))
    return v


def _tile_heads(g, n_heads):
    return jnp.tile(g.reshape(1, HEAD_DIM).astype(F32), (1, n_heads))


def kernel(x_prompt, x_sample, cache_k_win, cache_v_win, state_wkv, state_shift, bias_table, ln1_g, w_in,
           q_norm_g, k_norm_g, mu_shift, w0, w_lora2, a0, a_lora2, g_lora2, k_k, k_a, r_k, lnx_g, lnx_b,
           w_out, ln2_g, w_mlp1, w_mlp2):
    depth = w_in.shape[0]
    assert depth == 1, "a deeper stack would repeat the per-layer calls below"
    li = 0
    bp, tp, _ = x_prompt.shape
    bs, ts, _ = x_sample.shape
    keep = min(MAX_WINDOW, tp)

    wqkv = w_in[li][:, :3 * A_W].astype(BF16)
    wp = jnp.pad(w_in[li][:, 3 * A_W:], ((0, 0), (0, P_PAD - N_SHIFT))).astype(BF16)
    gq = _tile_heads(q_norm_g[li], H_ATTN)
    gk = _tile_heads(k_norm_g[li], H_ATTN)
    zrow = lambda n: jnp.zeros((n, B_W), F32)
    rwkv_weights = [
        _row(mu_shift[li], P_PAD), _row(w0[li]),
        jnp.concatenate([w_lora2[li].astype(F32), zrow(LANES - LORA_W)], axis=0).astype(BF16),
        _row(a0[li]),
        jnp.concatenate([zrow(LORA_W), a_lora2[li].astype(F32)], axis=0).astype(BF16),
        jnp.concatenate([g_lora2[li].astype(F32), zrow(LORA_PAD - LORA_G)], axis=0).astype(BF16),
        _row(k_k[li]), _row(k_a[li]), _row(r_k[li]), _row(lnx_g[li]), _row(lnx_b[li]),
        _block_diag(B_W, 1.0), _block_diag(B_W, 1.0 / HEAD_DIM),
    ]
    wo = w_out[li].astype(BF16)
    w1 = w_mlp1[li].astype(BF16)
    w2 = w_mlp2[li].astype(BF16)
    g1 = _row(ln1_g[li])
    g2 = _row(ln2_g[li])

    xp = x_prompt.reshape(bp * tp, D_MODEL)
    dils = tuple(d for _, d in BRANCHES)
    q, k, v, kf, vf, p, *views = _inproj(xp, g1, wqkv, wp, gq, gk, tm=512, seq=tp, keep=keep,
                                         dilations=dils[1:])
    views = [q, k, v] + views
    os_, lses = [], []
    for i, d in enumerate(dils):
        qd, kd, vd = (a.reshape(bp, tp // d, d * A_W) for a in views[3 * i:3 * i + 3])
        o, lse = _attn_branch(qd, kd, vd, _branch_bias(bias_table, d), d, tq=1024)
        os_.append(o.reshape(bp * tp // d, d * A_W))
        lses.append(lse.reshape(bp * tp // d, d * A_W))
    p3 = p.reshape(bp, tp, P_PAD)
    rw, s_p = _rwkv(p3, jnp.zeros((bp, 1, P_PAD), F32), jnp.zeros((bp, N_PAIRS, PAIR, PAIR), F32),
                    rwkv_weights, chunk=64, tile=256, seqs=2)
    y_p = _outmlp(xp, os_, lses, dils, rw.reshape(bp * tp, B_W), wo, g2, w1, w2, tm=512)
    heads = lambda a, b_, t_: a.reshape(b_, t_, H_ATTN, HEAD_DIM)
    k_win_p = heads(kf, bp, keep)[None]
    v_win_p = heads(vf, bp, keep)[None]
    wkv_p = _unpair_states(s_p)[None]
    shift_p = p3[:, -1, :N_SHIFT][None]

    xs = x_sample.reshape(bs * ts, D_MODEL)
    qs, _, _, kfs, vfs, ps = _inproj(xs, g1, wqkv, wp, gq, gk, tm=512)
    head_major = lambda a: a.reshape(bs, ts, H_ATTN, HEAD_DIM).transpose(0, 2, 1, 3)
    cache_t = lambda c: c[li].transpose(0, 2, 3, 1)
    att_s = _sample_attn(head_major(qs.astype(F32)), head_major(kfs), head_major(vfs),
                         cache_t(cache_k_win), cache_t(cache_v_win), bias_table)
    att_s = att_s.transpose(0, 2, 1, 3)
    ps3 = ps.reshape(bs, ts, P_PAD)
    shift0 = jnp.pad(state_shift[li].astype(F32), ((0, 0), (0, P_PAD - N_SHIFT))).reshape(bs, 1, P_PAD)
    rw_s, s_s = _rwkv(ps3, shift0, _pair_states(state_wkv[li].astype(F32)), rwkv_weights, chunk=ts, tile=ts,
                      seqs=8)
    y_s = _outmlp(xs, [att_s.reshape(bs * ts, A_W)], [], (1,), rw_s.reshape(bs * ts, B_W), wo, g2, w1, w2, tm=256)

    return (y_p.reshape(bp, tp, D_MODEL), y_s.reshape(bs, ts, D_MODEL), k_win_p, v_win_p, wkv_p, shift_p,
            heads(kfs, bs, ts)[None], heads(vfs, bs, ts)[None], _unpair_states(s_s)[None],
            ps3[:, -1, :N_SHIFT][None])
```

```python
import functools
import math

import numpy as np
import jax
import jax.numpy as jnp
from jax import lax
from jax.experimental import pallas as pl
from jax.experimental.pallas import tpu as pltpu

F32 = jnp.float32
BF16 = jnp.bfloat16

D_MODEL = 1024
HEAD_DIM = 64
A_W = 512
B_W = 512
H_ATTN = A_W // HEAD_DIM
H_RWKV = B_W // HEAD_DIM
BRANCHES = ((128, 1), (512, 4), (2048, 16))
MAX_WINDOW = 2048
WIN = 128
NUM_BUCKETS = 32
MAX_DISTANCE = 2048
LORA_W = 64
LORA_A = 64
LORA_G = 160
N_SHIFT = 3 * B_W + LORA_W + LORA_A + LORA_G
D_FF = 4 * D_MODEL
RMS_EPS = 1e-6
GN_EPS = 64e-5
NEG_INF = -1e30

LANES = 128
PAIR = 2 * HEAD_DIM
N_PAIRS = B_W // PAIR
P_PAD = 1920
LORA_PAD = P_PAD - 3 * B_W - LANES
VMEM_LIMIT = 56 * 1024 * 1024
ATTN_BLOCKS_PER_STEP = 8


def _dot(a, b):
    return jnp.dot(a.astype(BF16), b.astype(BF16), preferred_element_type=F32)


def _dot_nt(a, b):
    return lax.dot_general(a.astype(BF16), b.astype(BF16), (((1,), (1,)), ((), ())),
                           preferred_element_type=F32)


def _dot_tn(a, b):
    return lax.dot_general(a.astype(BF16), b.astype(BF16), (((0,), (0,)), ((), ())),
                           preferred_element_type=F32)


def _split2(x):
    hi = x.astype(BF16)
    lo = (x - hi.astype(F32)).astype(BF16)
    return hi, lo


def _split3(x):
    hi = x.astype(BF16)
    r1 = x - hi.astype(F32)
    mid = r1.astype(BF16)
    lo = (r1 - mid.astype(F32)).astype(BF16)
    return hi, mid, lo


def _group_reduce(x, bd, split=True):
    hi, lo = _split2(x)
    out = jnp.dot(hi, bd, preferred_element_type=F32)
    return out + jnp.dot(lo, bd, preferred_element_type=F32) if split else out


def _block_diag(n, value):
    idx = np.arange(n) // HEAD_DIM
    return jnp.asarray(np.where(idx[:, None] == idx[None, :], value, 0.0), BF16)


def _full(shape):
    return pl.BlockSpec(shape, lambda *_: (0,) * len(shape))


def _params(sem):
    return pltpu.CompilerParams(dimension_semantics=sem, vmem_limit_bytes=VMEM_LIMIT)


def _inproj_kernel(x_ref, g1_ref, wqkv_ref, wp_ref, gq_ref, gk_ref, bd_ref, *refs, dilations):
    n_view = 3 * len(dilations)
    q_ref, k_ref, v_ref, kf_ref, vf_ref, p_ref = refs[:6]
    view_refs = refs[6:6 + n_view]
    stage = refs[6 + n_view:]
    x = x_ref[...]
    ms = jnp.mean(x * x, axis=-1, keepdims=True)
    n = (x * lax.rsqrt(ms + RMS_EPS) * g1_ref[...]).astype(BF16)
    qkv = jnp.dot(n, wqkv_ref[...], preferred_element_type=F32)
    q = qkv[:, :A_W]
    k = qkv[:, A_W:2 * A_W]
    v = qkv[:, 2 * A_W:]
    bd = bd_ref[...]
    qn = q * lax.rsqrt(_group_reduce(q * q, bd, split=False) + RMS_EPS) * gq_ref[...]
    kn = k * lax.rsqrt(_group_reduce(k * k, bd, split=False) + RMS_EPS) * gk_ref[...]
    qs = qn * (HEAD_DIM ** -0.5)
    q_ref[...] = qs.astype(BF16)
    k_ref[...] = kn.astype(BF16)
    v_ref[...] = v.astype(BF16)
    kf_ref[...] = kn
    vf_ref[...] = v
    p_ref[...] = jnp.dot(n, wp_ref[...], preferred_element_type=F32)
    if dilations:
        tm = x.shape[0]
        for a, st in zip((qs, kn, v), stage):
            for j in range(A_W // LANES):
                st[j] = a[:, j * LANES:(j + 1) * LANES]
        for di, d in enumerate(dilations):
            for ai, st in enumerate(stage):
                out = view_refs[3 * di + ai]
                for r in range(d):
                    for j in range(A_W // LANES):
                        lo = r * A_W + j * LANES
                        out[:, lo:lo + LANES] = st[j, pl.ds(r, tm // d, stride=d), :].astype(BF16)


def _inproj(x, g1, wqkv, wp, gq, gk, tm, seq=None, keep=None, dilations=()):
    n_tok = x.shape[0]
    tm = min(tm, n_tok)
    seq = n_tok if seq is None else seq
    keep = seq if keep is None else keep
    tiles, kept, skip = seq // tm, keep // tm, (seq - keep) // tm
    row = lambda w: pl.BlockSpec((tm, w), lambda i: (i, 0))
    tail = pl.BlockSpec((tm, A_W), lambda i: ((i // tiles) * kept + jnp.maximum(i % tiles - skip, 0), 0))
    view_specs = [pl.BlockSpec((tm // d, d * A_W), lambda i: (i, 0)) for d in dilations for _ in range(3)]
    view_shapes = [jax.ShapeDtypeStruct((n_tok // d, d * A_W), BF16) for d in dilations for _ in range(3)]
    return pl.pallas_call(
        functools.partial(_inproj_kernel, dilations=dilations),
        grid=(n_tok // tm,),
        in_specs=[row(D_MODEL), _full((1, D_MODEL)), _full((D_MODEL, 3 * A_W)), _full((D_MODEL, P_PAD)),
                  _full((1, A_W)), _full((1, A_W)), _full((A_W, A_W))],
        out_specs=[row(A_W), row(A_W), row(A_W), tail, tail, row(P_PAD)] + view_specs,
        out_shape=[jax.ShapeDtypeStruct((n_tok, A_W), BF16)] * 3
        + [jax.ShapeDtypeStruct((n_tok // seq * keep, A_W), F32)] * 2
        + [jax.ShapeDtypeStruct((n_tok, P_PAD), F32)] + view_shapes,
        scratch_shapes=[pltpu.VMEM((A_W // LANES, tm, LANES), F32)] * (3 if dilations else 0),
        compiler_params=_params(("arbitrary",)),
        name="inproj",
    )(x, g1, wqkv, wp, gq, gk, _block_diag(A_W, 1.0 / HEAD_DIM))


def _head_masks():
    lane = lax.broadcasted_iota(jnp.int32, (1, PAIR), 1)
    return lane < HEAD_DIM, lane >= HEAD_DIM


def _attn_kernel(q_ref, kh_ref, kc_ref, vh_ref, vc_ref, bias_ref, o_ref, lse_ref, kbuf, vbuf, *, tq):
    i = pl.program_id(2)
    kbuf[0:WIN, :] = kh_ref[0]
    kbuf[WIN:, :] = kc_ref[0]
    vbuf[0:WIN, :] = vh_ref[0]
    vbuf[WIN:, :] = vc_ref[0]
    masks = _head_masks()

    blocks = min(ATTN_BLOCKS_PER_STEP, tq // WIN)

    def body(j, carry):
        offs = [pl.multiple_of((j * blocks + u) * WIN, WIN) for u in range(blocks)]
        first = jnp.logical_and(i == 0, j == 0).astype(jnp.int32)
        pair = lambda h: slice((h // 2) * PAIR, (h // 2 + 1) * PAIR)

        def scores(u, h):
            q2 = q_ref[0, pl.ds(offs[u], WIN), pair(h)]
            qm = jnp.where(masks[h % 2], q2, jnp.zeros_like(q2))
            return _dot_nt(qm, kbuf[pl.ds(offs[u], 2 * WIN), pair(h)]) + bias_ref[first if u == 0 else 0, h]

        s_all = [[scores(u, h) for h in range(H_ATTN)] for u in range(blocks)]
        ones = [jnp.broadcast_to(jnp.where(mk, 1.0, 0.0).astype(BF16), (2 * WIN, PAIR)) for mk in masks]
        for u in range(blocks):
            for pr in range(N_PAIRS):
                es, ms = [], []
                for h in (2 * pr, 2 * pr + 1):
                    s = s_all[u][h]
                    m = jnp.max(s, axis=-1, keepdims=True)
                    es.append(jnp.exp(s - m).astype(BF16))
                    ms.append(m)
                v2 = vbuf[pl.ds(offs[u], 2 * WIN), pair(2 * pr)]
                rhs = jnp.concatenate(
                    [jnp.concatenate([jnp.where(mk, v2, jnp.zeros_like(v2)), on], axis=1)
                     for mk, on in zip(masks, ones)], axis=0)
                res = jnp.dot(jnp.concatenate(es, axis=1), rhs, preferred_element_type=F32)
                l_pair = res[:, PAIR:]
                o_ref[0, pl.ds(offs[u], WIN), pair(2 * pr)] = (res[:, :PAIR] / l_pair).astype(BF16)
                lse_ref[0, pl.ds(offs[u], WIN), pair(2 * pr)] = (jnp.where(masks[0], ms[0], ms[1])
                                                                 + jnp.log(l_pair))
        return carry

    lax.fori_loop(0, tq // (WIN * blocks), body, 0)


def _attn_branch(q, k, v, bias, d, tq):
    b, l, _ = q.shape
    tq = min(tq, l)
    sub = tq // WIN
    cur = pl.BlockSpec((1, tq, A_W), lambda bi, r, i: (bi, i, r))
    halo = pl.BlockSpec((1, WIN, A_W), lambda bi, r, i: (bi, jnp.maximum(i * sub - 1, 0), r))
    return pl.pallas_call(
        functools.partial(_attn_kernel, tq=tq),
        grid=(b, d, l // tq),
        in_specs=[cur, halo, cur, halo, cur, _full((2, H_ATTN, WIN, 2 * WIN))],
        out_specs=[cur, cur],
        out_shape=[jax.ShapeDtypeStruct((b, l, d * A_W), BF16), jax.ShapeDtypeStruct((b, l, d * A_W), F32)],
        scratch_shapes=[pltpu.VMEM((tq + WIN, A_W), BF16), pltpu.VMEM((tq + WIN, A_W), BF16)],
        compiler_params=_params(("parallel", "parallel", "arbitrary")),
        name=f"attn_d{d}",
    )(q, k, k, v, v, bias)


def _t5_bucket(dist):
    dist = np.maximum(dist, 0)
    max_exact = NUM_BUCKETS // 2
    large = max_exact + (np.log(np.maximum(dist, 1) / max_exact)
                         / math.log(MAX_DISTANCE / max_exact)
                         * (NUM_BUCKETS - max_exact)).astype(np.int32)
    large = np.minimum(large, NUM_BUCKETS - 1)
    return np.where(dist < max_exact, dist, large).astype(np.int32)


def _bucket_bias(bias_table, dist):
    onehot = (_t5_bucket(dist).reshape(-1, 1) == np.arange(NUM_BUCKETS)[None, :])
    rows = jnp.dot(jnp.asarray(onehot, bias_table.dtype), bias_table, precision=lax.Precision.HIGHEST)
    return rows.T.reshape((bias_table.shape[1],) + dist.shape).astype(F32)


def _branch_bias(bias_table, d):
    m = np.arange(WIN)[:, None]
    n = np.arange(2 * WIN)[None, :]
    steps = m + WIN - n
    valid = (steps >= 0) & (steps <= WIN)
    bias = _bucket_bias(bias_table, steps * d)
    normal = jnp.where(jnp.asarray(valid), bias, NEG_INF)
    first = jnp.where(jnp.asarray(valid & (n >= WIN)), bias, NEG_INF)
    return jnp.stack([normal, first])


def _sample_attn_kernel(q_ref, kn_ref, vn_ref, kc_ref, vc_ref, bias_ref, biasn_ref, o_ref):
    heads = range(H_ATTN)
    q = [q_ref[0, h].astype(BF16) for h in heads]
    s_c = [jnp.dot(q[h], kc_ref[0, h].astype(BF16), preferred_element_type=F32) + bias_ref[h] for h in heads]
    s_n = [_dot_nt(q[h], kn_ref[0, h]) + biasn_ref[h] for h in heads]
    for h in heads:
        m = jnp.maximum(jnp.max(s_c[h], axis=-1, keepdims=True), jnp.max(s_n[h], axis=-1, keepdims=True))
        e_c = jnp.exp(s_c[h] - m)
        e_n = jnp.exp(s_n[h] - m)
        l = jnp.sum(e_c, axis=-1, keepdims=True) + jnp.sum(e_n, axis=-1, keepdims=True)
        o = _dot_nt(e_c, vc_ref[0, h]) + _dot(e_n, vn_ref[0, h])
        o_ref[0, h] = o / l


def _sample_bias(bias_table, lw, tq):
    i = np.arange(tq)[:, None]
    pos = np.arange(lw + tq)[None, :]
    dist = lw + i - pos
    count = np.zeros(dist.shape, np.int32)
    for w, d in BRANCHES:
        count += ((dist >= 0) & (dist <= w) & (dist % d == 0)).astype(np.int32)
    bias = _bucket_bias(bias_table, dist)
    logc = jnp.log(jnp.asarray(np.maximum(count, 1)).astype(bias_table.dtype)).astype(F32)
    full = jnp.where(jnp.asarray(count > 0), bias + logc, NEG_INF)
    return full[:, :, :lw], full[:, :, lw:]


def _sample_attn(q, kn, vn, kc_t, vc_t, bias_table):
    b, _, tq, _ = q.shape
    lw = kc_t.shape[3]
    new = pl.BlockSpec((1, H_ATTN, tq, HEAD_DIM), lambda bi: (bi, 0, 0, 0))
    cache = pl.BlockSpec((1, H_ATTN, HEAD_DIM, lw), lambda bi: (bi, 0, 0, 0))
    bias_c, bias_n = _sample_bias(bias_table, lw, tq)
    return pl.pallas_call(
        _sample_attn_kernel,
        grid=(b,),
        in_specs=[new, new, new, cache, cache, _full((H_ATTN, tq, lw)), _full((H_ATTN, tq, tq))],
        out_specs=new,
        out_shape=jax.ShapeDtypeStruct((b, H_ATTN, tq, HEAD_DIM), F32),
        compiler_params=_params(("parallel",)),
        name="sample_attn",
    )(q, kn, vn, kc_t, vc_t, bias_c, bias_n)


def _softplus(x):
    return jnp.maximum(x, 0.0) + jnp.log(1.0 + jnp.exp(-jnp.abs(x)))


def _rwkv_state_free(r, k, v, kk, a, cum, logw, last, c):
    tt = r.shape[0]
    nc = tt // c
    b = kk * a
    w_inv = jnp.exp(-cum)
    w_rem = jnp.exp(last - cum)
    rt = r * jnp.exp(cum)
    at = -kk * jnp.exp(cum - logw)
    bt = b * w_inv
    kt = k * w_inv
    bw = b * w_rem
    kw = k * w_rem
    w_end = jnp.exp(last)

    m0, m1 = _head_masks()
    lane2 = lax.broadcasted_iota(jnp.int32, (1, 2 * PAIR), 1)
    m0w = jnp.where(lane2 >= PAIR, lane2 - PAIR, lane2) < HEAD_DIM
    lane_c = lax.broadcasted_iota(jnp.int32, (c, 2 * c), 1)
    row_c = lax.broadcasted_iota(jnp.int32, (c, 2 * c), 0)
    left = lane_c < c
    col_c = jnp.where(left, lane_c, lane_c - c)
    strict = col_c < row_c
    incl = col_c <= row_c
    bd_c = ((lax.broadcasted_iota(jnp.int32, (2 * c, 2 * c), 0) >= c)
            == (lax.broadcasted_iota(jnp.int32, (2 * c, 2 * c), 1) >= c))
    bd_f = ((lax.broadcasted_iota(jnp.int32, (PAIR, PAIR), 0) >= HEAD_DIM)
            == (lax.broadcasted_iota(jnp.int32, (PAIR, PAIR), 1) >= HEAD_DIM))
    zeros = jnp.zeros((c, PAIR), BF16)

    items = [(n, pr) for n in range(nc) for pr in range(N_PAIRS)]
    cut = lambda x, it: x[it[0] * c:(it[0] + 1) * c, it[1] * PAIR:(it[1] + 1) * PAIR]
    sel = lambda m, x: jnp.where(m, x, jnp.zeros_like(x))
    cat0 = lambda *xs: jnp.concatenate(xs, axis=0)
    cat1 = lambda *xs: jnp.concatenate(xs, axis=1)

    def block_diag(x):
        xb = x.astype(BF16)
        return sel(bd_c, cat0(xb, xb))

    at_i = [cut(at, it).astype(BF16) for it in items]
    rt_i = [cut(rt, it) for it in items]
    v_i = [cut(v, it).astype(BF16) for it in items]
    bk = [cat0(cut(bt, it), cut(kt, it)).astype(BF16) for it in items]
    kb = [cat0(cut(kt, it), cut(bt, it)).astype(BF16) for it in items]
    rtb = [x.astype(BF16) for x in rt_i]
    yield
    aa0 = [_dot_nt(cat0(sel(m0, a_), sel(m0, r_)), y_) for a_, r_, y_ in zip(at_i, rtb, bk)]
    yield
    aa1 = [_dot_nt(cat0(sel(m1, a_), sel(m1, r_)), y_) for a_, r_, y_ in zip(at_i, rtb, kb)]
    yield
    x = [jnp.where(strict, jnp.where(left, p0[:c], p1[:c]), 0.0) for p0, p1 in zip(aa0, aa1)]
    akp = [jnp.where(strict, jnp.where(left, p1[:c], p0[:c]), 0.0) for p0, p1 in zip(aa0, aa1)]
    arb = [jnp.where(incl, jnp.where(left, p0[c:], p1[c:]), 0.0) for p0, p1 in zip(aa0, aa1)]
    ark = [jnp.where(incl, jnp.where(left, p1[c:], p0[c:]), 0.0) for p0, p1 in zip(aa0, aa1)]

    tp = x
    n = c
    while n > 2:
        xd = [block_diag(xi) for xi in x]
        x = [_dot(xi, d) for xi, d in zip(x, xd)]
        yield
        xd = [block_diag(xi) for xi in x]
        tp = [t + xi + _dot(t, d) for t, xi, d in zip(tp, x, xd)]
        yield
        n //= 2

    akv = [_dot(m_, cat0(sel(m1, v_), sel(m0, v_))) for m_, v_ in zip(akp, v_i)]
    yield
    rhs2 = [cat1(cut(at, it), q_) for it, q_ in zip(items, akv)]
    split = lambda x_: cat0(sel(m0w, x_), sel(~m0w, x_))
    pq = [x_ + _dot(t, split(x_.astype(BF16))) for x_, t in zip(rhs2, tp)]
    yield
    gy = [_dot(cat1(b_, k_), cat0(split(x_.astype(BF16)), cat1(zeros, sel(m1, v_)), cat1(zeros, sel(m0, v_))))
          for b_, k_, x_, v_ in zip(arb, ark, pq, v_i)]
    yield
    pg = [cat0(x_[:, :PAIR], r_ + g_[:, :PAIR]) for x_, r_, g_ in zip(pq, rt_i, gy)]

    return [dict(pg=pg_, q=x_[:, PAIR:], y0=g_[:, PAIR:], v=v_, bkw=cat0(cut(bw, it), cut(kw, it)),
                 w_end=cut(w_end, it)[:1])
            for pg_, x_, g_, v_, it in zip(pg, pq, gy, v_i, items)]


def _rwkv_serial(chunks, states, c, per_seq):
    bd_f = ((lax.broadcasted_iota(jnp.int32, (PAIR, PAIR), 0) >= HEAD_DIM)
            == (lax.broadcasted_iota(jnp.int32, (PAIR, PAIR), 1) >= HEAD_DIM))
    states = list(states)
    n_chunks = len(chunks) // N_PAIRS
    y_rows = [None] * n_chunks
    for j in range(per_seq):
        which = [sq * per_seq + j for sq in range(n_chunks // per_seq)]
        data = [d for n in which for d in chunks[n * N_PAIRS:(n + 1) * N_PAIRS]]
        uy = [_dot_nt(d["pg"], s) for d, s in zip(data, states)]
        u = [uy_[:c] + d["q"] for uy_, d in zip(uy, data)]
        ys = [uy_[c:] + d["y0"] for uy_, d in zip(uy, data)]
        for i, n in enumerate(which):
            y_rows[n] = jnp.concatenate(ys[i * N_PAIRS:(i + 1) * N_PAIRS], axis=1)
        yield
        upd = [_dot_tn(jnp.concatenate([u_.astype(BF16), d["v"]], axis=0), d["bkw"]) for u_, d in zip(u, data)]
        states = [s * d["w_end"] + jnp.where(bd_f, x, 0.0) for s, d, x in zip(states, data, upd)]
        yield
    return jnp.concatenate(y_rows, axis=0), states


def _interleave(*gens):
    results = [None] * len(gens)
    live = list(range(len(gens)))
    while live:
        for i in list(live):
            try:
                next(gens[i])
            except StopIteration as done:
                results[i] = done.value
                live.remove(i)
    return results


def _rwkv_kernel(p_ref, shift0_ref, s0_ref, mu_ref, w0_ref, wl_ref, a0_ref, al_ref, gl_ref,
                 kk_ref, ka_ref, rk_ref, lng_ref, lnb_ref, bd1_ref, bdm_ref,
                 y_ref, s_ref, carry, *, c, groups):
    ci = pl.program_id(1)
    seqs, tile = p_ref.shape[0], p_ref.shape[1]
    per = seqs // groups
    tt = per * tile
    bd1 = bd1_ref[...]

    @pl.when(ci == 0)
    def _():
        carry[...] = shift0_ref[...]
        s_ref[...] = s0_ref[...]

    def prepare(first):
        p = jnp.concatenate([p_ref[first + sq] for sq in range(per)], axis=0)
        row = lax.broadcasted_iota(jnp.int32, (tt, 1), 0)
        prev = pltpu.roll(p, 1, 0)
        for sq in range(per):
            prev = jnp.where(row == sq * tile, carry[first + sq], prev)
            carry[first + sq] = p[(sq + 1) * tile - 1:(sq + 1) * tile, :]
        xs = p + (prev - p) * mu_ref[...]
        yield
        r = xs[:, :B_W]
        k = xs[:, B_W:2 * B_W]
        v = xs[:, 2 * B_W:3 * B_W]
        xwa = xs[:, 3 * B_W:3 * B_W + LANES]
        xg = xs[:, 3 * B_W + LANES:]
        w_log = -_softplus(-(w0_ref[...] + _dot(jnp.tanh(xwa), wl_ref[...]))) - 0.5
        logw = -jnp.exp(w_log)
        yield
        a = jax.nn.sigmoid(a0_ref[...] + _dot(xwa, al_ref[...]))
        yield
        g = _dot(jax.nn.sigmoid(xg), gl_ref[...])
        yield
        kk = k * kk_ref[...]
        kk = kk / jnp.maximum(jnp.sqrt(_group_reduce(kk * kk, bd1)), 1e-12)
        yield
        k = k * (1.0 + (a - 1.0) * ka_ref[...])

        grp = tile if tile >= LANES else tt
        row_t = lax.broadcasted_iota(jnp.int32, (grp, grp), 0)
        col_t = lax.broadcasted_iota(jnp.int32, (grp, grp), 1)
        tri = jnp.where((col_t <= row_t) & (col_t >= (row_t // c) * c), 1.0, 0.0).astype(BF16)
        parts = _split3(logw)
        cum = jnp.concatenate(
            [sum(jnp.dot(tri, x[i * grp:(i + 1) * grp], preferred_element_type=F32) for x in parts)
             for i in range(tt // grp)], axis=0)
        yield
        cum3 = cum.reshape(tt // c, c, B_W)
        last = jnp.broadcast_to(cum3[:, c - 1:c, :], cum3.shape).reshape(tt, B_W)
        return (r, k, v, kk, a, cum, logw, last), (r, k, v, g)

    bdm = bdm_ref[...]

    def finish(gi, y, states, r, k, v, g):
        for sq in range(per):
            for pr in range(N_PAIRS):
                s_ref[gi * per + sq, pr] = states[sq * N_PAIRS + pr]
        dev = y - _group_reduce(y, bdm)
        yn = dev * lax.rsqrt(_group_reduce(dev * dev, bdm) + GN_EPS) * lng_ref[...] + lnb_ref[...]
        bonus = _group_reduce(r * k * rk_ref[...], bd1) * v
        out = (yn + bonus) * g
        for sq in range(per):
            y_ref[gi * per + sq] = out[sq * tile:(sq + 1) * tile].astype(BF16)

    inputs, chunks, extras = {}, {}, {}
    for t in range(groups + 2):
        stages, tags = [], []
        if t < groups:
            stages.append(prepare(t * per))
            tags.append(("prepare", t))
        if 0 <= t - 1 < groups:
            stages.append(_rwkv_state_free(*inputs[t - 1], c))
            tags.append(("free", t - 1))
        if 0 <= t - 2 < groups:
            gi = t - 2
            states = [s_ref[gi * per + sq, pr] for sq in range(per) for pr in range(N_PAIRS)]
            stages.append(_rwkv_serial(chunks[gi], states, c, tile // c))
            tags.append(("serial", gi))
        for (kind, gi), res in zip(tags, _interleave(*stages)):
            if kind == "prepare":
                inputs[gi], extras[gi] = res
            elif kind == "free":
                chunks[gi] = res
            else:
                finish(gi, *res, *extras[gi])


def _rwkv(p, shift0, s0, weights, chunk, tile, seqs, groups):
    b, t, _ = p.shape
    seqs = math.gcd(seqs, b)
    wspecs = [_full(w.shape) for w in weights]
    return pl.pallas_call(
        functools.partial(_rwkv_kernel, c=chunk, groups=math.gcd(groups, seqs)),
        grid=(b // seqs, t // tile),
        in_specs=[pl.BlockSpec((seqs, tile, P_PAD), lambda bi, ci: (bi, ci, 0)),
                  pl.BlockSpec((seqs, 1, P_PAD), lambda bi, ci: (bi, 0, 0)),
                  pl.BlockSpec((seqs, N_PAIRS, PAIR, PAIR), lambda bi, ci: (bi, 0, 0, 0))] + wspecs,
        out_specs=[pl.BlockSpec((seqs, tile, B_W), lambda bi, ci: (bi, ci, 0)),
                   pl.BlockSpec((seqs, N_PAIRS, PAIR, PAIR), lambda bi, ci: (bi, 0, 0, 0))],
        out_shape=[jax.ShapeDtypeStruct((b, t, B_W), BF16),
                   jax.ShapeDtypeStruct((b, N_PAIRS, PAIR, PAIR), F32)],
        scratch_shapes=[pltpu.VMEM((seqs, 1, P_PAD), F32)],
        compiler_params=_params(("parallel", "arbitrary")),
        name="rwkv7",
    )(p, shift0, s0, *weights)


def _pair_states(s):
    b = s.shape[0]
    s = s.reshape(b, N_PAIRS, 2, HEAD_DIM, HEAD_DIM)
    z = jnp.zeros_like(s[:, :, 0])
    top = jnp.concatenate([s[:, :, 0], z], axis=-1)
    bot = jnp.concatenate([z, s[:, :, 1]], axis=-1)
    return jnp.concatenate([top, bot], axis=-2)


def _unpair_states(s):
    return jnp.stack([s[:, :, :HEAD_DIM, :HEAD_DIM], s[:, :, HEAD_DIM:, HEAD_DIM:]], axis=2).reshape(
        s.shape[0], H_RWKV, HEAD_DIM, HEAD_DIM)


def _outmlp_kernel(*refs, dilations):
    n_branch = len(dilations)
    x_ref = refs[0]
    o_refs = refs[1:1 + n_branch]
    lse_refs = refs[1 + n_branch:1 + 2 * n_branch] if n_branch > 1 else ()
    n_in = 1 + n_branch + len(lse_refs)
    rw_ref, wo_ref, g2_ref, w1_ref, w2_ref, y_ref = refs[n_in:n_in + 6]
    stage = refs[n_in + 6:]
    tm = x_ref.shape[0]

    def natural(ref, d, st):
        if d == 1:
            return ref[...].astype(F32)
        for r in range(d):
            for j in range(A_W // LANES):
                lo = r * A_W + j * LANES
                st[j, pl.ds(r, tm // d, stride=d), :] = ref[:, lo:lo + LANES].astype(F32)
        return jnp.concatenate([st[j] for j in range(A_W // LANES)], axis=1)

    if n_branch > 1:
        lses = [natural(ref, d, stage[2 * i]) for i, (ref, d) in enumerate(zip(lse_refs, dilations))]
        top = functools.reduce(jnp.maximum, lses)
        num = den = None
        for i, (o_ref, d, lse) in enumerate(zip(o_refs, dilations, lses)):
            e = jnp.exp(lse - top)
            t = e * natural(o_ref, d, stage[2 * i + 1])
            num = t if num is None else num + t
            den = e if den is None else den + e
        att = num / den
    else:
        att = o_refs[0][...]
    h = x_ref[...] + _dot(att, wo_ref[:A_W, :]) + jnp.dot(rw_ref[...], wo_ref[A_W:, :],
                                                           preferred_element_type=F32)
    ms = jnp.mean(h * h, axis=-1, keepdims=True)
    m = (h * lax.rsqrt(ms + RMS_EPS) * g2_ref[...]).astype(BF16)
    acc = None
    for c in range(D_FF // D_MODEL):
        sl = slice(c * D_MODEL, (c + 1) * D_MODEL)
        u = jnp.maximum(jnp.dot(m, w1_ref[:, sl], preferred_element_type=F32), 0.0)
        t = jnp.dot((u * u).astype(BF16), w2_ref[sl, :], preferred_element_type=F32)
        acc = t if acc is None else acc + t
    y_ref[...] = h + acc


def _outmlp(x, os_, lses, dilations, rw, wo, g2, w1, w2, tm):
    n_tok = x.shape[0]
    tm = min(tm, n_tok)
    row = lambda w: pl.BlockSpec((tm, w), lambda i: (i, 0))
    view = lambda d: pl.BlockSpec((tm // d, d * A_W), lambda i: (i, 0))
    const = lambda shape: pl.BlockSpec(shape, lambda i: (0, 0), pipeline_mode=pl.Buffered(1))
    n_stage = 2 * len(dilations) if len(dilations) > 1 else 0
    return pl.pallas_call(
        functools.partial(_outmlp_kernel, dilations=dilations),
        grid=(n_tok // tm,),
        in_specs=[row(D_MODEL)] + [view(d) for d in dilations] + [view(d) for d in dilations[:len(lses)]]
        + [row(B_W), const((D_MODEL, D_MODEL)), const((1, D_MODEL)), const((D_MODEL, D_FF)),
           const((D_FF, D_MODEL))],
        out_specs=row(D_MODEL),
        out_shape=jax.ShapeDtypeStruct((n_tok, D_MODEL), F32),
        scratch_shapes=[pltpu.VMEM((A_W // LANES, tm, LANES), F32)] * n_stage,
        compiler_params=_params(("parallel",)),
        name="outmlp",
    )(x, *os_, *lses, rw, wo, g2, w1, w2)


def _row(v, width=None):
    v = v.reshape(1, -1).astype(F32)
    if width is not None and v.shape[1] < width:
        v = jnp.pad(v, ((0, 0), (0, width - v.shape[1])))
    return v


def _tile_heads(g, n_heads):
    return jnp.tile(g.reshape(1, HEAD_DIM).astype(F32), (1, n_heads))


def kernel(x_prompt, x_sample, cache_k_win, cache_v_win, state_wkv, state_shift, bias_table, ln1_g, w_in,
           q_norm_g, k_norm_g, mu_shift, w0, w_lora2, a0, a_lora2, g_lora2, k_k, k_a, r_k, lnx_g, lnx_b,
           w_out, ln2_g, w_mlp1, w_mlp2):
    depth = w_in.shape[0]
    assert depth == 1, "a deeper stack would repeat the per-layer calls below"
    li = 0
    bp, tp, _ = x_prompt.shape
    bs, ts, _ = x_sample.shape
    keep = min(MAX_WINDOW, tp)

    wqkv = w_in[li][:, :3 * A_W].astype(BF16)
    wp = jnp.pad(w_in[li][:, 3 * A_W:], ((0, 0), (0, P_PAD - N_SHIFT))).astype(BF16)
    gq = _tile_heads(q_norm_g[li], H_ATTN)
    gk = _tile_heads(k_norm_g[li], H_ATTN)
    zrow = lambda n: jnp.zeros((n, B_W), F32)
    rwkv_weights = [
        _row(mu_shift[li], P_PAD), _row(w0[li]),
        jnp.concatenate([w_lora2[li].astype(F32), zrow(LANES - LORA_W)], axis=0).astype(BF16),
        _row(a0[li]),
        jnp.concatenate([zrow(LORA_W), a_lora2[li].astype(F32)], axis=0).astype(BF16),
        jnp.concatenate([g_lora2[li].astype(F32), zrow(LORA_PAD - LORA_G)], axis=0).astype(BF16),
        _row(k_k[li]), _row(k_a[li]), _row(r_k[li]), _row(lnx_g[li]), _row(lnx_b[li]),
        _block_diag(B_W, 1.0), _block_diag(B_W, 1.0 / HEAD_DIM),
    ]
    wo = w_out[li].astype(BF16)
    w1 = w_mlp1[li].astype(BF16)
    w2 = w_mlp2[li].astype(BF16)
    g1 = _row(ln1_g[li])
    g2 = _row(ln2_g[li])

    xp = x_prompt.reshape(bp * tp, D_MODEL)
    dils = tuple(d for _, d in BRANCHES)
    q, k, v, kf, vf, p, *views = _inproj(xp, g1, wqkv, wp, gq, gk, tm=512, seq=tp, keep=keep,
                                         dilations=dils[1:])
    views = [q, k, v] + views
    os_, lses = [], []
    for i, d in enumerate(dils):
        qd, kd, vd = (a.reshape(bp, tp // d, d * A_W) for a in views[3 * i:3 * i + 3])
        o, lse = _attn_branch(qd, kd, vd, _branch_bias(bias_table, d), d, tq=1024)
        os_.append(o.reshape(bp * tp // d, d * A_W))
        lses.append(lse.reshape(bp * tp // d, d * A_W))
    p3 = p.reshape(bp, tp, P_PAD)
    rw, s_p = _rwkv(p3, jnp.zeros((bp, 1, P_PAD), F32), jnp.zeros((bp, N_PAIRS, PAIR, PAIR), F32),
                    rwkv_weights, chunk=64, tile=256, seqs=4, groups=1)
    y_p = _outmlp(xp, os_, lses, dils, rw.reshape(bp * tp, B_W), wo, g2, w1, w2, tm=512)
    heads = lambda a, b_, t_: a.reshape(b_, t_, H_ATTN, HEAD_DIM)
    k_win_p = heads(kf, bp, keep)[None]
    v_win_p = heads(vf, bp, keep)[None]
    wkv_p = _unpair_states(s_p)[None]
    shift_p = p3[:, -1, :N_SHIFT][None]

    xs = x_sample.reshape(bs * ts, D_MODEL)
    qs, _, _, kfs, vfs, ps = _inproj(xs, g1, wqkv, wp, gq, gk, tm=512)
    head_major = lambda a: a.reshape(bs, ts, H_ATTN, HEAD_DIM).transpose(0, 2, 1, 3)
    cache_t = lambda c: c[li].transpose(0, 2, 3, 1)
    att_s = _sample_attn(head_major(qs.astype(F32)), head_major(kfs), head_major(vfs),
                         cache_t(cache_k_win), cache_t(cache_v_win), bias_table)
    att_s = att_s.transpose(0, 2, 1, 3)
    ps3 = ps.reshape(bs, ts, P_PAD)
    shift0 = jnp.pad(state_shift[li].astype(F32), ((0, 0), (0, P_PAD - N_SHIFT))).reshape(bs, 1, P_PAD)
    rw_s, s_s = _rwkv(ps3, shift0, _pair_states(state_wkv[li].astype(F32)), rwkv_weights, chunk=ts, tile=ts,
                      seqs=8, groups=1)
    y_s = _outmlp(xs, [att_s.reshape(bs * ts, A_W)], [], (1,), rw_s.reshape(bs * ts, B_W), wo, g2, w1, w2, tm=256)

    return (y_p.reshape(bp, tp, D_MODEL), y_s.reshape(bs, ts, D_MODEL), k_win_p, v_win_p, wkv_p, shift_p,
            heads(kfs, bs, ts)[None], heads(vfs, bs, ts)[None], _unpair_states(s_s)[None],
            ps3[:, -1, :N_SHIFT][None])
```

```python
import functools
import math

import numpy as np
import jax
import jax.numpy as jnp
from jax import lax
from jax.experimental import pallas as pl
from jax.experimental.pallas import tpu as pltpu

F32 = jnp.float32
BF16 = jnp.bfloat16

D_MODEL = 1024
HEAD_DIM = 64
A_W = 512
B_W = 512
H_ATTN = A_W // HEAD_DIM
H_RWKV = B_W // HEAD_DIM
BRANCHES = ((128, 1), (512, 4), (2048, 16))
MAX_WINDOW = 2048
WIN = 128
NUM_BUCKETS = 32
MAX_DISTANCE = 2048
LORA_W = 64
LORA_A = 64
LORA_G = 160
N_SHIFT = 3 * B_W + LORA_W + LORA_A + LORA_G
D_FF = 4 * D_MODEL
RMS_EPS = 1e-6
GN_EPS = 64e-5
NEG_INF = -1e30

LANES = 128
PAIR = 2 * HEAD_DIM
N_PAIRS = B_W // PAIR
P_PAD = 1920
LORA_PAD = P_PAD - 3 * B_W - LANES
VMEM_LIMIT = 56 * 1024 * 1024
ATTN_BLOCKS_PER_STEP = 8


def _dot(a, b):
    return jnp.dot(a.astype(BF16), b.astype(BF16), preferred_element_type=F32)


def _dot_nt(a, b):
    return lax.dot_general(a.astype(BF16), b.astype(BF16), (((1,), (1,)), ((), ())),
                           preferred_element_type=F32)


def _dot_tn(a, b):
    return lax.dot_general(a.astype(BF16), b.astype(BF16), (((0,), (0,)), ((), ())),
                           preferred_element_type=F32)


def _split2(x):
    hi = x.astype(BF16)
    lo = (x - hi.astype(F32)).astype(BF16)
    return hi, lo


def _split3(x):
    hi = x.astype(BF16)
    r1 = x - hi.astype(F32)
    mid = r1.astype(BF16)
    lo = (r1 - mid.astype(F32)).astype(BF16)
    return hi, mid, lo


def _group_reduce(x, bd, split=True):
    hi, lo = _split2(x)
    out = jnp.dot(hi, bd, preferred_element_type=F32)
    return out + jnp.dot(lo, bd, preferred_element_type=F32) if split else out


def _block_diag(n, value):
    idx = np.arange(n) // HEAD_DIM
    return jnp.asarray(np.where(idx[:, None] == idx[None, :], value, 0.0), BF16)


def _full(shape):
    return pl.BlockSpec(shape, lambda *_: (0,) * len(shape))


def _params(sem):
    return pltpu.CompilerParams(dimension_semantics=sem, vmem_limit_bytes=VMEM_LIMIT)


def _inproj_kernel(x_ref, g1_ref, wqkv_ref, wp_ref, gq_ref, gk_ref, bd_ref, *refs, dilations, transpose_kv):
    n_view = 3 * len(dilations)
    q_ref, k_ref, v_ref, kf_ref, vf_ref, p_ref = refs[:6]
    view_refs = refs[6:6 + n_view]
    stage = refs[6 + n_view:]
    x = x_ref[...]
    ms = jnp.mean(x * x, axis=-1, keepdims=True)
    n = (x * lax.rsqrt(ms + RMS_EPS) * g1_ref[...]).astype(BF16)
    p_ref[...] = jnp.dot(n, wp_ref[...], preferred_element_type=F32)
    qkv = jnp.dot(n, wqkv_ref[...], preferred_element_type=F32)
    q = qkv[:, :A_W]
    k = qkv[:, A_W:2 * A_W]
    v = qkv[:, 2 * A_W:]
    bd = bd_ref[...]
    qn = q * lax.rsqrt(_group_reduce(q * q, bd, split=False) + RMS_EPS) * gq_ref[...]
    kn = k * lax.rsqrt(_group_reduce(k * k, bd, split=False) + RMS_EPS) * gk_ref[...]
    qs = qn * (HEAD_DIM ** -0.5)
    q_ref[...] = qs.astype(BF16)
    k_ref[...] = kn.astype(BF16)
    v_ref[...] = v.astype(BF16)
    if transpose_kv:
        kf_ref[0] = kn.T
        vf_ref[0] = v.T
    else:
        kf_ref[...] = kn
        vf_ref[...] = v
    if dilations:
        tm = x.shape[0]
        for a, st in zip((qs, kn, v), stage):
            for j in range(A_W // LANES):
                st[j] = a[:, j * LANES:(j + 1) * LANES]
        for di, d in enumerate(dilations):
            for ai, st in enumerate(stage):
                out = view_refs[3 * di + ai]
                for r in range(d):
                    for j in range(A_W // LANES):
                        lo = r * A_W + j * LANES
                        out[:, lo:lo + LANES] = st[j, pl.ds(r, tm // d, stride=d), :].astype(BF16)


def _inproj(x, g1, wqkv, wp, gq, gk, tm, seq=None, keep=None, dilations=(), transpose_kv=False):
    n_tok = x.shape[0]
    tm = min(tm, n_tok)
    seq = n_tok if seq is None else seq
    keep = seq if keep is None else keep
    tiles, kept, skip = seq // tm, keep // tm, (seq - keep) // tm
    row = lambda w: pl.BlockSpec((tm, w), lambda i: (i, 0))
    if transpose_kv:
        tail = pl.BlockSpec((1, A_W, tm), lambda i: (i // tiles, 0, jnp.maximum(i % tiles - skip, 0)))
        tail_shape = jax.ShapeDtypeStruct((n_tok // seq, A_W, keep), F32)
    else:
        tail = pl.BlockSpec((tm, A_W), lambda i: ((i // tiles) * kept + jnp.maximum(i % tiles - skip, 0), 0))
        tail_shape = jax.ShapeDtypeStruct((n_tok // seq * keep, A_W), F32)
    view_specs = [pl.BlockSpec((tm // d, d * A_W), lambda i: (i, 0)) for d in dilations for _ in range(3)]
    view_shapes = [jax.ShapeDtypeStruct((n_tok // d, d * A_W), BF16) for d in dilations for _ in range(3)]
    return pl.pallas_call(
        functools.partial(_inproj_kernel, dilations=dilations, transpose_kv=transpose_kv),
        grid=(n_tok // tm,),
        in_specs=[row(D_MODEL), _full((1, D_MODEL)), _full((D_MODEL, 3 * A_W)), _full((D_MODEL, P_PAD)),
                  _full((1, A_W)), _full((1, A_W)), _full((A_W, A_W))],
        out_specs=[row(A_W), row(A_W), row(A_W), tail, tail, row(P_PAD)] + view_specs,
        out_shape=[jax.ShapeDtypeStruct((n_tok, A_W), BF16)] * 3
        + [tail_shape] * 2
        + [jax.ShapeDtypeStruct((n_tok, P_PAD), F32)] + view_shapes,
        scratch_shapes=[pltpu.VMEM((A_W // LANES, tm, LANES), F32)] * (3 if dilations else 0),
        compiler_params=_params(("arbitrary",)),
        name="inproj",
    )(x, g1, wqkv, wp, gq, gk, _block_diag(A_W, 1.0 / HEAD_DIM))


def _head_masks():
    lane = lax.broadcasted_iota(jnp.int32, (1, PAIR), 1)
    return lane < HEAD_DIM, lane >= HEAD_DIM


def _attn_kernel(q_ref, kh_ref, kc_ref, vh_ref, vc_ref, bias_ref, o_ref, lse_ref, kbuf, vbuf, *, tq):
    i = pl.program_id(2)
    kbuf[0:WIN, :] = kh_ref[0]
    kbuf[WIN:, :] = kc_ref[0]
    vbuf[0:WIN, :] = vh_ref[0]
    vbuf[WIN:, :] = vc_ref[0]
    masks = _head_masks()

    blocks = min(ATTN_BLOCKS_PER_STEP, tq // WIN)

    def body(j, carry):
        offs = [pl.multiple_of((j * blocks + u) * WIN, WIN) for u in range(blocks)]
        first = jnp.logical_and(i == 0, j == 0).astype(jnp.int32)
        pair = lambda h: slice((h // 2) * PAIR, (h // 2 + 1) * PAIR)

        def scores(u, h):
            q2 = q_ref[0, pl.ds(offs[u], WIN), pair(h)]
            qm = jnp.where(masks[h % 2], q2, jnp.zeros_like(q2))
            return _dot_nt(qm, kbuf[pl.ds(offs[u], 2 * WIN), pair(h)]) + bias_ref[first if u == 0 else 0, h]

        s_all = [[scores(u, h) for h in range(H_ATTN)] for u in range(blocks)]
        ones = [jnp.broadcast_to(jnp.where(mk, 1.0, 0.0).astype(BF16), (2 * WIN, PAIR)) for mk in masks]
        for u in range(blocks):
            for pr in range(N_PAIRS):
                es, ms = [], []
                for h in (2 * pr, 2 * pr + 1):
                    s = s_all[u][h]
                    m = jnp.max(s, axis=-1, keepdims=True)
                    es.append(jnp.exp(s - m).astype(BF16))
                    ms.append(m)
                v2 = vbuf[pl.ds(offs[u], 2 * WIN), pair(2 * pr)]
                rhs = jnp.concatenate(
                    [jnp.concatenate([jnp.where(mk, v2, jnp.zeros_like(v2)), on], axis=1)
                     for mk, on in zip(masks, ones)], axis=0)
                res = jnp.dot(jnp.concatenate(es, axis=1), rhs, preferred_element_type=F32)
                l_pair = res[:, PAIR:]
                o_ref[0, pl.ds(offs[u], WIN), pair(2 * pr)] = (res[:, :PAIR] / l_pair).astype(BF16)
                lse_ref[0, pl.ds(offs[u], WIN), pair(2 * pr)] = (jnp.where(masks[0], ms[0], ms[1])
                                                                 + jnp.log(l_pair))
        return carry

    lax.fori_loop(0, tq // (WIN * blocks), body, 0)


def _attn_branch(q, k, v, bias, d, tq):
    b, l, _ = q.shape
    tq = min(tq, l)
    sub = tq // WIN
    cur = pl.BlockSpec((1, tq, A_W), lambda bi, r, i: (bi, i, r))
    halo = pl.BlockSpec((1, WIN, A_W), lambda bi, r, i: (bi, jnp.maximum(i * sub - 1, 0), r))
    return pl.pallas_call(
        functools.partial(_attn_kernel, tq=tq),
        grid=(b, d, l // tq),
        in_specs=[cur, halo, cur, halo, cur, _full((2, H_ATTN, WIN, 2 * WIN))],
        out_specs=[cur, cur],
        out_shape=[jax.ShapeDtypeStruct((b, l, d * A_W), BF16), jax.ShapeDtypeStruct((b, l, d * A_W), F32)],
        scratch_shapes=[pltpu.VMEM((tq + WIN, A_W), BF16), pltpu.VMEM((tq + WIN, A_W), BF16)],
        compiler_params=_params(("parallel", "parallel", "arbitrary")),
        name=f"attn_d{d}",
    )(q, k, k, v, v, bias)


def _t5_bucket(dist):
    dist = np.maximum(dist, 0)
    max_exact = NUM_BUCKETS // 2
    large = max_exact + (np.log(np.maximum(dist, 1) / max_exact)
                         / math.log(MAX_DISTANCE / max_exact)
                         * (NUM_BUCKETS - max_exact)).astype(np.int32)
    large = np.minimum(large, NUM_BUCKETS - 1)
    return np.where(dist < max_exact, dist, large).astype(np.int32)


def _bucket_bias(bias_table, dist):
    onehot = (_t5_bucket(dist).reshape(-1, 1) == np.arange(NUM_BUCKETS)[None, :])
    rows = jnp.dot(jnp.asarray(onehot, bias_table.dtype), bias_table, precision=lax.Precision.HIGHEST)
    return rows.T.reshape((bias_table.shape[1],) + dist.shape).astype(F32)


def _branch_bias(bias_table, d):
    m = np.arange(WIN)[:, None]
    n = np.arange(2 * WIN)[None, :]
    steps = m + WIN - n
    valid = (steps >= 0) & (steps <= WIN)
    bias = _bucket_bias(bias_table, steps * d)
    normal = jnp.where(jnp.asarray(valid), bias, NEG_INF)
    first = jnp.where(jnp.asarray(valid & (n >= WIN)), bias, NEG_INF)
    return jnp.stack([normal, first])


def _sample_attn_kernel(q_ref, kn_ref, vn_ref, kc_ref, vc_ref, bias_ref, biasn_ref, o_ref):
    heads = range(H_ATTN)
    q = [q_ref[0, h].astype(BF16) for h in heads]
    s_c = [jnp.dot(q[h], kc_ref[0, h].astype(BF16), preferred_element_type=F32) + bias_ref[h] for h in heads]
    s_n = [_dot_nt(q[h], kn_ref[0, h]) + biasn_ref[h] for h in heads]
    for h in heads:
        m = jnp.maximum(jnp.max(s_c[h], axis=-1, keepdims=True), jnp.max(s_n[h], axis=-1, keepdims=True))
        e_c = jnp.exp(s_c[h] - m)
        e_n = jnp.exp(s_n[h] - m)
        l = jnp.sum(e_c, axis=-1, keepdims=True) + jnp.sum(e_n, axis=-1, keepdims=True)
        o = _dot_nt(e_c, vc_ref[0, h]) + _dot(e_n, vn_ref[0, h])
        o_ref[0, h] = o / l


def _sample_bias(bias_table, lw, tq):
    i = np.arange(tq)[:, None]
    pos = np.arange(lw + tq)[None, :]
    dist = lw + i - pos
    count = np.zeros(dist.shape, np.int32)
    for w, d in BRANCHES:
        count += ((dist >= 0) & (dist <= w) & (dist % d == 0)).astype(np.int32)
    bias = _bucket_bias(bias_table, dist)
    logc = jnp.log(jnp.asarray(np.maximum(count, 1)).astype(bias_table.dtype)).astype(F32)
    full = jnp.where(jnp.asarray(count > 0), bias + logc, NEG_INF)
    return full[:, :, :lw], full[:, :, lw:]


def _sample_attn(q, kn, vn, kc_t, vc_t, bias_table):
    b, _, tq, _ = q.shape
    lw = kc_t.shape[3]
    new = pl.BlockSpec((1, H_ATTN, tq, HEAD_DIM), lambda bi: (bi, 0, 0, 0))
    cache = pl.BlockSpec((1, H_ATTN, HEAD_DIM, lw), lambda bi: (bi, 0, 0, 0))
    bias_c, bias_n = _sample_bias(bias_table, lw, tq)
    return pl.pallas_call(
        _sample_attn_kernel,
        grid=(b,),
        in_specs=[new, new, new, cache, cache, _full((H_ATTN, tq, lw)), _full((H_ATTN, tq, tq))],
        out_specs=new,
        out_shape=jax.ShapeDtypeStruct((b, H_ATTN, tq, HEAD_DIM), F32),
        compiler_params=_params(("parallel",)),
        name="sample_attn",
    )(q, kn, vn, kc_t, vc_t, bias_c, bias_n)


def _softplus(x):
    return jnp.maximum(x, 0.0) + jnp.log(1.0 + jnp.exp(-jnp.abs(x)))


def _rwkv_state_free(r, k, v, kk, a, cum, logw, last, c):
    tt = r.shape[0]
    nc = tt // c
    b = kk * a
    w_inv = jnp.exp(-cum)
    w_rem = jnp.exp(last - cum)
    rt = r * jnp.exp(cum)
    at = -kk * jnp.exp(cum - logw)
    bt = b * w_inv
    kt = k * w_inv
    bw = b * w_rem
    kw = k * w_rem

    m0, m1 = _head_masks()
    lane2 = lax.broadcasted_iota(jnp.int32, (1, 2 * PAIR), 1)
    m0w = jnp.where(lane2 >= PAIR, lane2 - PAIR, lane2) < HEAD_DIM
    lane_c = lax.broadcasted_iota(jnp.int32, (c, 2 * c), 1)
    row_c = lax.broadcasted_iota(jnp.int32, (c, 2 * c), 0)
    left = lane_c < c
    col_c = jnp.where(left, lane_c, lane_c - c)
    strict = col_c < row_c
    incl = col_c <= row_c
    bd_c = ((lax.broadcasted_iota(jnp.int32, (2 * c, 2 * c), 0) >= c)
            == (lax.broadcasted_iota(jnp.int32, (2 * c, 2 * c), 1) >= c))
    bd_f = ((lax.broadcasted_iota(jnp.int32, (PAIR, PAIR), 0) >= HEAD_DIM)
            == (lax.broadcasted_iota(jnp.int32, (PAIR, PAIR), 1) >= HEAD_DIM))
    zeros = jnp.zeros((c, PAIR), BF16)

    items = [(n, pr) for n in range(nc) for pr in range(N_PAIRS)]
    cut = lambda x, it: x[it[0] * c:(it[0] + 1) * c, it[1] * PAIR:(it[1] + 1) * PAIR]
    sel = lambda m, x: jnp.where(m, x, jnp.zeros_like(x))
    cat0 = lambda *xs: jnp.concatenate(xs, axis=0)
    cat1 = lambda *xs: jnp.concatenate(xs, axis=1)

    def block_diag(x):
        xb = x.astype(BF16)
        return sel(bd_c, cat0(xb, xb))

    at_i = [cut(at, it).astype(BF16) for it in items]
    rt_i = [cut(rt, it) for it in items]
    v_i = [cut(v, it).astype(BF16) for it in items]
    bk = [cat0(cut(bt, it), cut(kt, it)).astype(BF16) for it in items]
    kb = [cat0(cut(kt, it), cut(bt, it)).astype(BF16) for it in items]
    rtb = [x.astype(BF16) for x in rt_i]
    yield
    aa0 = [_dot_nt(cat0(sel(m0, a_), sel(m0, r_)), y_) for a_, r_, y_ in zip(at_i, rtb, bk)]
    yield
    aa1 = [_dot_nt(cat0(sel(m1, a_), sel(m1, r_)), y_) for a_, r_, y_ in zip(at_i, rtb, kb)]
    yield
    x = [jnp.where(strict, jnp.where(left, p0[:c], p1[:c]), 0.0) for p0, p1 in zip(aa0, aa1)]
    akp = [jnp.where(strict, jnp.where(left, p1[:c], p0[:c]), 0.0) for p0, p1 in zip(aa0, aa1)]
    arb = [jnp.where(incl, jnp.where(left, p0[c:], p1[c:]), 0.0) for p0, p1 in zip(aa0, aa1)]
    ark = [jnp.where(incl, jnp.where(left, p1[c:], p0[c:]), 0.0) for p0, p1 in zip(aa0, aa1)]

    tp = x
    n = c
    while n > 2:
        xd = [block_diag(xi) for xi in x]
        x = [_dot(xi, d) for xi, d in zip(x, xd)]
        yield
        xd = [block_diag(xi) for xi in x]
        tp = [t + xi + _dot(t, d) for t, xi, d in zip(tp, x, xd)]
        yield
        n //= 2

    akv = [_dot(m_, cat0(sel(m1, v_), sel(m0, v_))) for m_, v_ in zip(akp, v_i)]
    yield
    rhs2 = [cat1(cut(at, it), q_) for it, q_ in zip(items, akv)]
    split = lambda x_: cat0(sel(m0w, x_), sel(~m0w, x_))
    pq = [x_ + _dot(t, split(x_.astype(BF16))) for x_, t in zip(rhs2, tp)]
    yield
    gy = [_dot(cat1(b_, k_), cat0(split(x_.astype(BF16)), cat1(zeros, sel(m1, v_)), cat1(zeros, sel(m0, v_))))
          for b_, k_, x_, v_ in zip(arb, ark, pq, v_i)]
    yield
    pg = [cat0(x_[:, :PAIR], r_ + g_[:, :PAIR]) for x_, r_, g_ in zip(pq, rt_i, gy)]

    return [dict(pg=pg_, q=x_[:, PAIR:], y0=g_[:, PAIR:], v=v_, bkw=cat0(cut(bw, it), cut(kw, it)),
                 w_end=jnp.exp(cut(last, it)[:1]))
            for pg_, x_, g_, v_, it in zip(pg, pq, gy, v_i, items)]


def _rwkv_serial(chunks, states, c, per_seq):
    bd_f = ((lax.broadcasted_iota(jnp.int32, (PAIR, PAIR), 0) >= HEAD_DIM)
            == (lax.broadcasted_iota(jnp.int32, (PAIR, PAIR), 1) >= HEAD_DIM))
    states = list(states)
    n_chunks = len(chunks) // N_PAIRS
    y_rows = [None] * n_chunks
    for j in range(per_seq):
        which = [sq * per_seq + j for sq in range(n_chunks // per_seq)]
        data = [d for n in which for d in chunks[n * N_PAIRS:(n + 1) * N_PAIRS]]
        uy = [_dot_nt(d["pg"], s) for d, s in zip(data, states)]
        u = [uy_[:c] + d["q"] for uy_, d in zip(uy, data)]
        ys = [uy_[c:] + d["y0"] for uy_, d in zip(uy, data)]
        for i, n in enumerate(which):
            y_rows[n] = jnp.concatenate(ys[i * N_PAIRS:(i + 1) * N_PAIRS], axis=1)
        yield
        upd = [_dot_tn(jnp.concatenate([u_.astype(BF16), d["v"]], axis=0), d["bkw"]) for u_, d in zip(u, data)]
        states = [s * d["w_end"] + jnp.where(bd_f, x, 0.0) for s, d, x in zip(states, data, upd)]
        yield
    return jnp.concatenate(y_rows, axis=0), states


def _interleave(*gens):
    results = [None] * len(gens)
    live = list(range(len(gens)))
    while live:
        for i in list(live):
            try:
                next(gens[i])
            except StopIteration as done:
                results[i] = done.value
                live.remove(i)
    return results


def _rwkv_kernel(p_ref, shift0_ref, s0_ref, mu_ref, w0_ref, wl_ref, a0_ref, al_ref, gl_ref,
                 kk_ref, ka_ref, rk_ref, lng_ref, lnb_ref, bd1_ref, bdm_ref,
                 y_ref, s_ref, carry, *, c, groups):
    ci = pl.program_id(1)
    seqs, tile = p_ref.shape[0], p_ref.shape[1]
    per = seqs // groups
    tt = per * tile
    bd1 = bd1_ref[...]

    @pl.when(ci == 0)
    def _():
        carry[...] = shift0_ref[...]
        s_ref[...] = s0_ref[...]

    def prepare(first):
        p = jnp.concatenate([p_ref[first + sq] for sq in range(per)], axis=0)
        row = lax.broadcasted_iota(jnp.int32, (tt, 1), 0)
        prev = pltpu.roll(p, 1, 0)
        for sq in range(per):
            prev = jnp.where(row == sq * tile, carry[first + sq], prev)
            carry[first + sq] = p[(sq + 1) * tile - 1:(sq + 1) * tile, :]
        xs = p + (prev - p) * mu_ref[...]
        yield
        r = xs[:, :B_W]
        k = xs[:, B_W:2 * B_W]
        v = xs[:, 2 * B_W:3 * B_W]
        xwa = xs[:, 3 * B_W:3 * B_W + LANES]
        xg = xs[:, 3 * B_W + LANES:]
        w_log = -_softplus(-(w0_ref[...] + _dot(jnp.tanh(xwa), wl_ref[...]))) - 0.5
        logw = -jnp.exp(w_log)
        yield
        a = jax.nn.sigmoid(a0_ref[...] + _dot(xwa, al_ref[...]))
        yield
        g = _dot(jax.nn.sigmoid(xg), gl_ref[...])
        yield
        kk = k * kk_ref[...]
        kk = kk / jnp.maximum(jnp.sqrt(_group_reduce(kk * kk, bd1)), 1e-12)
        yield
        k = k * (1.0 + (a - 1.0) * ka_ref[...])

        grp = tile if tile >= LANES else tt
        row_t = lax.broadcasted_iota(jnp.int32, (grp, grp), 0)
        col_t = lax.broadcasted_iota(jnp.int32, (grp, grp), 1)
        tri = jnp.where((col_t <= row_t) & (col_t >= (row_t // c) * c), 1.0, 0.0).astype(BF16)
        parts = _split3(logw)
        cum = jnp.concatenate(
            [sum(jnp.dot(tri, x[i * grp:(i + 1) * grp], preferred_element_type=F32) for x in parts)
             for i in range(tt // grp)], axis=0)
        yield
        cum3 = cum.reshape(tt // c, c, B_W)
        last = jnp.broadcast_to(cum3[:, c - 1:c, :], cum3.shape).reshape(tt, B_W)
        return (r, k, v, kk, a, cum, logw, last), (r, k, v, g)

    bdm = bdm_ref[...]

    def finish(gi, y, states, r, k, v, g):
        for sq in range(per):
            for pr in range(N_PAIRS):
                s_ref[gi * per + sq, pr] = states[sq * N_PAIRS + pr]
        dev = y - _group_reduce(y, bdm)
        yn = dev * lax.rsqrt(_group_reduce(dev * dev, bdm) + GN_EPS) * lng_ref[...] + lnb_ref[...]
        bonus = _group_reduce(r * k * rk_ref[...], bd1) * v
        out = (yn + bonus) * g
        for sq in range(per):
            y_ref[gi * per + sq] = out[sq * tile:(sq + 1) * tile].astype(BF16)

    inputs, chunks, extras = {}, {}, {}
    for t in range(groups + 2):
        stages, tags = [], []
        if t < groups:
            stages.append(prepare(t * per))
            tags.append(("prepare", t))
        if 0 <= t - 1 < groups:
            stages.append(_rwkv_state_free(*inputs[t - 1], c))
            tags.append(("free", t - 1))
        if 0 <= t - 2 < groups:
            gi = t - 2
            states = [s_ref[gi * per + sq, pr] for sq in range(per) for pr in range(N_PAIRS)]
            stages.append(_rwkv_serial(chunks[gi], states, c, tile // c))
            tags.append(("serial", gi))
        for (kind, gi), res in zip(tags, _interleave(*stages)):
            if kind == "prepare":
                inputs[gi], extras[gi] = res
            elif kind == "free":
                chunks[gi] = res
            else:
                finish(gi, *res, *extras[gi])


def _rwkv(p, shift0, s0, weights, chunk, tile, seqs, groups):
    b, t, _ = p.shape
    seqs = math.gcd(seqs, b)
    wspecs = [_full(w.shape) for w in weights]
    return pl.pallas_call(
        functools.partial(_rwkv_kernel, c=chunk, groups=math.gcd(groups, seqs)),
        grid=(b // seqs, t // tile),
        in_specs=[pl.BlockSpec((seqs, tile, P_PAD), lambda bi, ci: (bi, ci, 0)),
                  pl.BlockSpec((seqs, 1, P_PAD), lambda bi, ci: (bi, 0, 0)),
                  pl.BlockSpec((seqs, N_PAIRS, PAIR, PAIR), lambda bi, ci: (bi, 0, 0, 0))] + wspecs,
        out_specs=[pl.BlockSpec((seqs, tile, B_W), lambda bi, ci: (bi, ci, 0)),
                   pl.BlockSpec((seqs, N_PAIRS, PAIR, PAIR), lambda bi, ci: (bi, 0, 0, 0))],
        out_shape=[jax.ShapeDtypeStruct((b, t, B_W), BF16),
                   jax.ShapeDtypeStruct((b, N_PAIRS, PAIR, PAIR), F32)],
        scratch_shapes=[pltpu.VMEM((seqs, 1, P_PAD), F32)],
        compiler_params=_params(("parallel", "arbitrary")),
        name="rwkv7",
    )(p, shift0, s0, *weights)


def _pair_states(s):
    b = s.shape[0]
    s = s.reshape(b, N_PAIRS, 2, HEAD_DIM, HEAD_DIM)
    z = jnp.zeros_like(s[:, :, 0])
    top = jnp.concatenate([s[:, :, 0], z], axis=-1)
    bot = jnp.concatenate([z, s[:, :, 1]], axis=-1)
    return jnp.concatenate([top, bot], axis=-2)


def _unpair_states(s):
    return jnp.stack([s[:, :, :HEAD_DIM, :HEAD_DIM], s[:, :, HEAD_DIM:, HEAD_DIM:]], axis=2).reshape(
        s.shape[0], H_RWKV, HEAD_DIM, HEAD_DIM)


def _outmlp_kernel(*refs, dilations):
    n_branch = len(dilations)
    x_ref = refs[0]
    o_refs = refs[1:1 + n_branch]
    lse_refs = refs[1 + n_branch:1 + 2 * n_branch] if n_branch > 1 else ()
    n_in = 1 + n_branch + len(lse_refs)
    rw_ref, wo_ref, g2_ref, w1_ref, w2_ref, y_ref = refs[n_in:n_in + 6]
    stage = refs[n_in + 6:]
    tm = x_ref.shape[0]

    def natural(ref, d, st):
        if d == 1:
            return ref[...].astype(F32)
        for r in range(d):
            for j in range(A_W // LANES):
                lo = r * A_W + j * LANES
                st[j, pl.ds(r, tm // d, stride=d), :] = ref[:, lo:lo + LANES].astype(F32)
        return jnp.concatenate([st[j] for j in range(A_W // LANES)], axis=1)

    if n_branch > 1:
        lses = [natural(ref, d, stage[2 * i]) for i, (ref, d) in enumerate(zip(lse_refs, dilations))]
        top = functools.reduce(jnp.maximum, lses)
        num = den = None
        for i, (o_ref, d, lse) in enumerate(zip(o_refs, dilations, lses)):
            e = jnp.exp(lse - top)
            t = e * natural(o_ref, d, stage[2 * i + 1])
            num = t if num is None else num + t
            den = e if den is None else den + e
        att = num / den
    else:
        att = o_refs[0][...]
    h = x_ref[...] + _dot(att, wo_ref[:A_W, :]) + jnp.dot(rw_ref[...], wo_ref[A_W:, :],
                                                           preferred_element_type=F32)
    ms = jnp.mean(h * h, axis=-1, keepdims=True)
    m = (h * lax.rsqrt(ms + RMS_EPS) * g2_ref[...]).astype(BF16)
    acc = None
    for c in range(D_FF // D_MODEL):
        sl = slice(c * D_MODEL, (c + 1) * D_MODEL)
        u = jnp.maximum(jnp.dot(m, w1_ref[:, sl], preferred_element_type=F32), 0.0)
        t = jnp.dot((u * u).astype(BF16), w2_ref[sl, :], preferred_element_type=F32)
        acc = t if acc is None else acc + t
    y_ref[...] = h + acc


def _outmlp(x, os_, lses, dilations, rw, wo, g2, w1, w2, tm):
    n_tok = x.shape[0]
    tm = min(tm, n_tok)
    row = lambda w: pl.BlockSpec((tm, w), lambda i: (i, 0))
    view = lambda d: pl.BlockSpec((tm // d, d * A_W), lambda i: (i, 0))
    const = lambda shape: pl.BlockSpec(shape, lambda i: (0, 0), pipeline_mode=pl.Buffered(1))
    n_stage = 2 * len(dilations) if len(dilations) > 1 else 0
    return pl.pallas_call(
        functools.partial(_outmlp_kernel, dilations=dilations),
        grid=(n_tok // tm,),
        in_specs=[row(D_MODEL)] + [view(d) for d in dilations] + [view(d) for d in dilations[:len(lses)]]
        + [row(B_W), const((D_MODEL, D_MODEL)), const((1, D_MODEL)), const((D_MODEL, D_FF)),
           const((D_FF, D_MODEL))],
        out_specs=row(D_MODEL),
        out_shape=jax.ShapeDtypeStruct((n_tok, D_MODEL), F32),
        scratch_shapes=[pltpu.VMEM((A_W // LANES, tm, LANES), F32)] * n_stage,
        compiler_params=_params(("parallel",)),
        name="outmlp",
    )(x, *os_, *lses, rw, wo, g2, w1, w2)


def _row(v, width=None):
    v = v.reshape(1, -1).astype(F32)
    if width is not None and v.shape[1] < width:
        v = jnp.pad(v, ((0, 0), (0, width - v.shape[1])))
    return v


def _tile_heads(g, n_heads):
    return jnp.tile(g.reshape(1, HEAD_DIM).astype(F32), (1, n_heads))


def kernel(x_prompt, x_sample, cache_k_win, cache_v_win, state_wkv, state_shift, bias_table, ln1_g, w_in,
           q_norm_g, k_norm_g, mu_shift, w0, w_lora2, a0, a_lora2, g_lora2, k_k, k_a, r_k, lnx_g, lnx_b,
           w_out, ln2_g, w_mlp1, w_mlp2):
    depth = w_in.shape[0]
    assert depth == 1, "a deeper stack would repeat the per-layer calls below"
    li = 0
    bp, tp, _ = x_prompt.shape
    bs, ts, _ = x_sample.shape
    keep = min(MAX_WINDOW, tp)

    wqkv = w_in[li][:, :3 * A_W].astype(BF16)
    wp = jnp.pad(w_in[li][:, 3 * A_W:], ((0, 0), (0, P_PAD - N_SHIFT))).astype(BF16)
    gq = _tile_heads(q_norm_g[li], H_ATTN)
    gk = _tile_heads(k_norm_g[li], H_ATTN)
    zrow = lambda n: jnp.zeros((n, B_W), F32)
    rwkv_weights = [
        _row(mu_shift[li], P_PAD), _row(w0[li]),
        jnp.concatenate([w_lora2[li].astype(F32), zrow(LANES - LORA_W)], axis=0).astype(BF16),
        _row(a0[li]),
        jnp.concatenate([zrow(LORA_W), a_lora2[li].astype(F32)], axis=0).astype(BF16),
        jnp.concatenate([g_lora2[li].astype(F32), zrow(LORA_PAD - LORA_G)], axis=0).astype(BF16),
        _row(k_k[li]), _row(k_a[li]), _row(r_k[li]), _row(lnx_g[li]), _row(lnx_b[li]),
        _block_diag(B_W, 1.0), _block_diag(B_W, 1.0 / HEAD_DIM),
    ]
    wo = w_out[li].astype(BF16)
    w1 = w_mlp1[li].astype(BF16)
    w2 = w_mlp2[li].astype(BF16)
    g1 = _row(ln1_g[li])
    g2 = _row(ln2_g[li])

    xp = x_prompt.reshape(bp * tp, D_MODEL)
    dils = tuple(d for _, d in BRANCHES)
    q, k, v, kf, vf, p, *views = _inproj(xp, g1, wqkv, wp, gq, gk, tm=512, seq=tp, keep=keep,
                                         dilations=dils[1:], transpose_kv=True)
    views = [q, k, v] + views
    os_, lses = [], []
    for i, d in enumerate(dils):
        qd, kd, vd = (a.reshape(bp, tp // d, d * A_W) for a in views[3 * i:3 * i + 3])
        o, lse = _attn_branch(qd, kd, vd, _branch_bias(bias_table, d), d, tq=1024)
        os_.append(o.reshape(bp * tp // d, d * A_W))
        lses.append(lse.reshape(bp * tp // d, d * A_W))
    p3 = p.reshape(bp, tp, P_PAD)
    rw, s_p = _rwkv(p3, jnp.zeros((bp, 1, P_PAD), F32), jnp.zeros((bp, N_PAIRS, PAIR, PAIR), F32),
                    rwkv_weights, chunk=64, tile=256, seqs=4, groups=1)
    y_p = _outmlp(xp, os_, lses, dils, rw.reshape(bp * tp, B_W), wo, g2, w1, w2, tm=512)
    heads = lambda a, b_, t_: a.reshape(b_, t_, H_ATTN, HEAD_DIM)
    window = lambda a: a.reshape(bp, H_ATTN, HEAD_DIM, keep).transpose(0, 3, 1, 2)[None]
    k_win_p = window(kf)
    v_win_p = window(vf)
    wkv_p = _unpair_states(s_p)[None]
    shift_p = p3[:, -1, :N_SHIFT][None]

    xs = x_sample.reshape(bs * ts, D_MODEL)
    qs, _, _, kfs, vfs, ps = _inproj(xs, g1, wqkv, wp, gq, gk, tm=512)
    head_major = lambda a: a.reshape(bs, ts, H_ATTN, HEAD_DIM).transpose(0, 2, 1, 3)
    cache_t = lambda c: c[li].transpose(0, 2, 3, 1)
    att_s = _sample_attn(head_major(qs.astype(F32)), head_major(kfs), head_major(vfs),
                         cache_t(cache_k_win), cache_t(cache_v_win), bias_table)
    att_s = att_s.transpose(0, 2, 1, 3)
    ps3 = ps.reshape(bs, ts, P_PAD)
    shift0 = jnp.pad(state_shift[li].astype(F32), ((0, 0), (0, P_PAD - N_SHIFT))).reshape(bs, 1, P_PAD)
    rw_s, s_s = _rwkv(ps3, shift0, _pair_states(state_wkv[li].astype(F32)), rwkv_weights, chunk=ts, tile=ts,
                      seqs=8, groups=1)
    y_s = _outmlp(xs, [att_s.reshape(bs * ts, A_W)], [], (1,), rw_s.reshape(bs * ts, B_W), wo, g2, w1, w2, tm=256)

    return (y_p.reshape(bp, tp, D_MODEL), y_s.reshape(bs, ts, D_MODEL), k_win_p, v_win_p, wkv_p, shift_p,
            heads(kfs, bs, ts)[None], heads(vfs, bs, ts)[None], _unpair_states(s_s)[None],
            ps3[:, -1, :N_SHIFT][None])
```

```python
import functools
import math

import numpy as np
import jax
import jax.numpy as jnp
from jax import lax
from jax.experimental import pallas as pl
from jax.experimental.pallas import tpu as pltpu

F32 = jnp.float32
BF16 = jnp.bfloat16

D_MODEL = 1024
HEAD_DIM = 64
A_W = 512
B_W = 512
H_ATTN = A_W // HEAD_DIM
H_RWKV = B_W // HEAD_DIM
BRANCHES = ((128, 1), (512, 4), (2048, 16))
MAX_WINDOW = 2048
WIN = 128
NUM_BUCKETS = 32
MAX_DISTANCE = 2048
LORA_W = 64
LORA_A = 64
LORA_G = 160
N_SHIFT = 3 * B_W + LORA_W + LORA_A + LORA_G
D_FF = 4 * D_MODEL
RMS_EPS = 1e-6
GN_EPS = 64e-5
NEG_INF = -1e30

LANES = 128
PAIR = 2 * HEAD_DIM
N_PAIRS = B_W // PAIR
P_PAD = 1920
LORA_PAD = P_PAD - 3 * B_W - LANES
VMEM_LIMIT = 56 * 1024 * 1024
ATTN_BLOCKS_PER_STEP = 8


def _dot(a, b):
    return jnp.dot(a.astype(BF16), b.astype(BF16), preferred_element_type=F32)


def _dot_nt(a, b):
    return lax.dot_general(a.astype(BF16), b.astype(BF16), (((1,), (1,)), ((), ())),
                           preferred_element_type=F32)


def _dot_tn(a, b):
    return lax.dot_general(a.astype(BF16), b.astype(BF16), (((0,), (0,)), ((), ())),
                           preferred_element_type=F32)


def _split2(x):
    hi = x.astype(BF16)
    lo = (x - hi.astype(F32)).astype(BF16)
    return hi, lo


def _split3(x):
    hi = x.astype(BF16)
    r1 = x - hi.astype(F32)
    mid = r1.astype(BF16)
    lo = (r1 - mid.astype(F32)).astype(BF16)
    return hi, mid, lo


def _group_reduce(x, bd, split=True):
    hi, lo = _split2(x)
    out = jnp.dot(hi, bd, preferred_element_type=F32)
    return out + jnp.dot(lo, bd, preferred_element_type=F32) if split else out


def _block_diag(n, value):
    idx = np.arange(n) // HEAD_DIM
    return jnp.asarray(np.where(idx[:, None] == idx[None, :], value, 0.0), BF16)


def _full(shape):
    return pl.BlockSpec(shape, lambda *_: (0,) * len(shape))


def _params(sem):
    return pltpu.CompilerParams(dimension_semantics=sem, vmem_limit_bytes=VMEM_LIMIT)


def _inproj_kernel(x_ref, g1_ref, wqkv_ref, wp_ref, gq_ref, gk_ref, bd_ref, *refs, dilations, transpose_kv):
    n_view = 3 * len(dilations)
    q_ref, k_ref, v_ref, kf_ref, vf_ref, p_ref = refs[:6]
    view_refs = refs[6:6 + n_view]
    stage = refs[6 + n_view:]
    x = x_ref[...]
    ms = jnp.mean(x * x, axis=-1, keepdims=True)
    n = (x * lax.rsqrt(ms + RMS_EPS) * g1_ref[...]).astype(BF16)
    p_ref[...] = jnp.dot(n, wp_ref[...], preferred_element_type=F32)
    qkv = jnp.dot(n, wqkv_ref[...], preferred_element_type=F32)
    q = qkv[:, :A_W]
    k = qkv[:, A_W:2 * A_W]
    v = qkv[:, 2 * A_W:]
    bd = bd_ref[...]
    qn = q * lax.rsqrt(_group_reduce(q * q, bd, split=False) + RMS_EPS) * gq_ref[...]
    kn = k * lax.rsqrt(_group_reduce(k * k, bd, split=False) + RMS_EPS) * gk_ref[...]
    qs = qn * (HEAD_DIM ** -0.5)
    q_ref[...] = qs.astype(BF16)
    k_ref[...] = kn.astype(BF16)
    v_ref[...] = v.astype(BF16)
    if transpose_kv:
        kf_ref[0] = kn.T
        vf_ref[0] = v.T
    else:
        kf_ref[...] = kn
        vf_ref[...] = v
    if dilations:
        tm = x.shape[0]
        for a, st in zip((qs, kn, v), stage):
            for j in range(A_W // LANES):
                st[j] = a[:, j * LANES:(j + 1) * LANES]
        for di, d in enumerate(dilations):
            for ai, st in enumerate(stage):
                out = view_refs[3 * di + ai]
                for r in range(d):
                    for j in range(A_W // LANES):
                        lo = r * A_W + j * LANES
                        out[:, lo:lo + LANES] = st[j, pl.ds(r, tm // d, stride=d), :].astype(BF16)


def _inproj(x, g1, wqkv, wp, gq, gk, tm, seq=None, keep=None, dilations=(), transpose_kv=False):
    n_tok = x.shape[0]
    tm = min(tm, n_tok)
    seq = n_tok if seq is None else seq
    keep = seq if keep is None else keep
    tiles, kept, skip = seq // tm, keep // tm, (seq - keep) // tm
    row = lambda w: pl.BlockSpec((tm, w), lambda i: (i, 0))
    if transpose_kv:
        tail = pl.BlockSpec((1, A_W, tm), lambda i: (i // tiles, 0, jnp.maximum(i % tiles - skip, 0)))
        tail_shape = jax.ShapeDtypeStruct((n_tok // seq, A_W, keep), F32)
    else:
        tail = pl.BlockSpec((tm, A_W), lambda i: ((i // tiles) * kept + jnp.maximum(i % tiles - skip, 0), 0))
        tail_shape = jax.ShapeDtypeStruct((n_tok // seq * keep, A_W), F32)
    view_specs = [pl.BlockSpec((tm // d, d * A_W), lambda i: (i, 0)) for d in dilations for _ in range(3)]
    view_shapes = [jax.ShapeDtypeStruct((n_tok // d, d * A_W), BF16) for d in dilations for _ in range(3)]
    return pl.pallas_call(
        functools.partial(_inproj_kernel, dilations=dilations, transpose_kv=transpose_kv),
        grid=(n_tok // tm,),
        in_specs=[row(D_MODEL), _full((1, D_MODEL)), _full((D_MODEL, 3 * A_W)), _full((D_MODEL, P_PAD)),
                  _full((1, A_W)), _full((1, A_W)), _full((A_W, A_W))],
        out_specs=[row(A_W), row(A_W), row(A_W), tail, tail, row(P_PAD)] + view_specs,
        out_shape=[jax.ShapeDtypeStruct((n_tok, A_W), BF16)] * 3
        + [tail_shape] * 2
        + [jax.ShapeDtypeStruct((n_tok, P_PAD), F32)] + view_shapes,
        scratch_shapes=[pltpu.VMEM((A_W // LANES, tm, LANES), F32)] * (3 if dilations else 0),
        compiler_params=_params(("arbitrary",)),
        name="inproj",
    )(x, g1, wqkv, wp, gq, gk, _block_diag(A_W, 1.0 / HEAD_DIM))


def _head_masks():
    lane = lax.broadcasted_iota(jnp.int32, (1, PAIR), 1)
    return lane < HEAD_DIM, lane >= HEAD_DIM


def _attn_kernel(q_ref, kh_ref, kc_ref, vh_ref, vc_ref, bias_ref, o_ref, lse_ref, kbuf, vbuf, *, tq):
    i = pl.program_id(2)
    kbuf[0:WIN, :] = kh_ref[0]
    kbuf[WIN:, :] = kc_ref[0]
    vbuf[0:WIN, :] = vh_ref[0]
    vbuf[WIN:, :] = vc_ref[0]
    masks = _head_masks()

    blocks = min(ATTN_BLOCKS_PER_STEP, tq // WIN)

    def body(j, carry):
        offs = [pl.multiple_of((j * blocks + u) * WIN, WIN) for u in range(blocks)]
        first = jnp.logical_and(i == 0, j == 0).astype(jnp.int32)
        pair = lambda h: slice((h // 2) * PAIR, (h // 2 + 1) * PAIR)

        def scores(u, h):
            q2 = q_ref[0, pl.ds(offs[u], WIN), pair(h)]
            qm = jnp.where(masks[h % 2], q2, jnp.zeros_like(q2))
            return _dot_nt(qm, kbuf[pl.ds(offs[u], 2 * WIN), pair(h)]) + bias_ref[first if u == 0 else 0, h]

        s_all = [[scores(u, h) for h in range(H_ATTN)] for u in range(blocks)]
        ones = [jnp.broadcast_to(jnp.where(mk, 1.0, 0.0).astype(BF16), (2 * WIN, PAIR)) for mk in masks]
        for u in range(blocks):
            for pr in range(N_PAIRS):
                es, ms = [], []
                for h in (2 * pr, 2 * pr + 1):
                    s = s_all[u][h]
                    m = jnp.max(s, axis=-1, keepdims=True)
                    es.append(jnp.exp((s - m).astype(BF16)))
                    ms.append(m)
                v2 = vbuf[pl.ds(offs[u], 2 * WIN), pair(2 * pr)]
                rhs = jnp.concatenate(
                    [jnp.concatenate([jnp.where(mk, v2, jnp.zeros_like(v2)), on], axis=1)
                     for mk, on in zip(masks, ones)], axis=0)
                res = jnp.dot(jnp.concatenate(es, axis=1), rhs, preferred_element_type=F32)
                l_pair = res[:, PAIR:]
                o_ref[0, pl.ds(offs[u], WIN), pair(2 * pr)] = (res[:, :PAIR] / l_pair).astype(BF16)
                lse_ref[0, pl.ds(offs[u], WIN), pair(2 * pr)] = (jnp.where(masks[0], ms[0], ms[1])
                                                                 + jnp.log(l_pair))
        return carry

    lax.fori_loop(0, tq // (WIN * blocks), body, 0)


def _attn_branch(q, k, v, bias, d, tq):
    b, l, _ = q.shape
    tq = min(tq, l)
    sub = tq // WIN
    cur = pl.BlockSpec((1, tq, A_W), lambda bi, r, i: (bi, i, r))
    halo = pl.BlockSpec((1, WIN, A_W), lambda bi, r, i: (bi, jnp.maximum(i * sub - 1, 0), r))
    return pl.pallas_call(
        functools.partial(_attn_kernel, tq=tq),
        grid=(b, d, l // tq),
        in_specs=[cur, halo, cur, halo, cur, _full((2, H_ATTN, WIN, 2 * WIN))],
        out_specs=[cur, cur],
        out_shape=[jax.ShapeDtypeStruct((b, l, d * A_W), BF16), jax.ShapeDtypeStruct((b, l, d * A_W), F32)],
        scratch_shapes=[pltpu.VMEM((tq + WIN, A_W), BF16), pltpu.VMEM((tq + WIN, A_W), BF16)],
        compiler_params=_params(("parallel", "parallel", "arbitrary")),
        name=f"attn_d{d}",
    )(q, k, k, v, v, bias)


def _t5_bucket(dist):
    dist = np.maximum(dist, 0)
    max_exact = NUM_BUCKETS // 2
    large = max_exact + (np.log(np.maximum(dist, 1) / max_exact)
                         / math.log(MAX_DISTANCE / max_exact)
                         * (NUM_BUCKETS - max_exact)).astype(np.int32)
    large = np.minimum(large, NUM_BUCKETS - 1)
    return np.where(dist < max_exact, dist, large).astype(np.int32)


def _bucket_bias(bias_table, dist):
    onehot = (_t5_bucket(dist).reshape(-1, 1) == np.arange(NUM_BUCKETS)[None, :])
    rows = jnp.dot(jnp.asarray(onehot, bias_table.dtype), bias_table, precision=lax.Precision.HIGHEST)
    return rows.T.reshape((bias_table.shape[1],) + dist.shape).astype(F32)


def _branch_bias(bias_table, d):
    m = np.arange(WIN)[:, None]
    n = np.arange(2 * WIN)[None, :]
    steps = m + WIN - n
    valid = (steps >= 0) & (steps <= WIN)
    bias = _bucket_bias(bias_table, steps * d)
    normal = jnp.where(jnp.asarray(valid), bias, NEG_INF)
    first = jnp.where(jnp.asarray(valid & (n >= WIN)), bias, NEG_INF)
    return jnp.stack([normal, first])


def _sample_attn_kernel(q_ref, kn_ref, vn_ref, kc_ref, vc_ref, bias_ref, biasn_ref, o_ref):
    heads = range(H_ATTN)
    q = [q_ref[0, h].astype(BF16) for h in heads]
    s_c = [jnp.dot(q[h], kc_ref[0, h].astype(BF16), preferred_element_type=F32) + bias_ref[h] for h in heads]
    s_n = [_dot_nt(q[h], kn_ref[0, h]) + biasn_ref[h] for h in heads]
    for h in heads:
        m = jnp.maximum(jnp.max(s_c[h], axis=-1, keepdims=True), jnp.max(s_n[h], axis=-1, keepdims=True))
        e_c = jnp.exp(s_c[h] - m)
        e_n = jnp.exp(s_n[h] - m)
        l = jnp.sum(e_c, axis=-1, keepdims=True) + jnp.sum(e_n, axis=-1, keepdims=True)
        o = _dot_nt(e_c, vc_ref[0, h]) + _dot(e_n, vn_ref[0, h])
        o_ref[0, h] = o / l


def _sample_bias(bias_table, lw, tq):
    i = np.arange(tq)[:, None]
    pos = np.arange(lw + tq)[None, :]
    dist = lw + i - pos
    count = np.zeros(dist.shape, np.int32)
    for w, d in BRANCHES:
        count += ((dist >= 0) & (dist <= w) & (dist % d == 0)).astype(np.int32)
    bias = _bucket_bias(bias_table, dist)
    logc = jnp.log(jnp.asarray(np.maximum(count, 1)).astype(bias_table.dtype)).astype(F32)
    full = jnp.where(jnp.asarray(count > 0), bias + logc, NEG_INF)
    return full[:, :, :lw], full[:, :, lw:]


def _sample_attn(q, kn, vn, kc_t, vc_t, bias_table):
    b, _, tq, _ = q.shape
    lw = kc_t.shape[3]
    new = pl.BlockSpec((1, H_ATTN, tq, HEAD_DIM), lambda bi: (bi, 0, 0, 0))
    cache = pl.BlockSpec((1, H_ATTN, HEAD_DIM, lw), lambda bi: (bi, 0, 0, 0))
    bias_c, bias_n = _sample_bias(bias_table, lw, tq)
    return pl.pallas_call(
        _sample_attn_kernel,
        grid=(b,),
        in_specs=[new, new, new, cache, cache, _full((H_ATTN, tq, lw)), _full((H_ATTN, tq, tq))],
        out_specs=new,
        out_shape=jax.ShapeDtypeStruct((b, H_ATTN, tq, HEAD_DIM), F32),
        compiler_params=_params(("parallel",)),
        name="sample_attn",
    )(q, kn, vn, kc_t, vc_t, bias_c, bias_n)


def _softplus(x):
    return jnp.maximum(x, 0.0) + jnp.log(1.0 + jnp.exp(-jnp.abs(x)))


def _rwkv_state_free(r, k, v, kk, a, cum, logw, last, c):
    tt = r.shape[0]
    nc = tt // c
    b = kk * a
    w_inv = jnp.exp(-cum)
    w_rem = jnp.exp(last - cum)
    rt = r * jnp.exp(cum)
    at = -kk * jnp.exp(cum - logw)
    bt = b * w_inv
    kt = k * w_inv
    bw = b * w_rem
    kw = k * w_rem

    m0, m1 = _head_masks()
    lane2 = lax.broadcasted_iota(jnp.int32, (1, 2 * PAIR), 1)
    m0w = jnp.where(lane2 >= PAIR, lane2 - PAIR, lane2) < HEAD_DIM
    lane_c = lax.broadcasted_iota(jnp.int32, (c, 2 * c), 1)
    row_c = lax.broadcasted_iota(jnp.int32, (c, 2 * c), 0)
    left = lane_c < c
    col_c = jnp.where(left, lane_c, lane_c - c)
    strict = col_c < row_c
    incl = col_c <= row_c
    bd_c = ((lax.broadcasted_iota(jnp.int32, (2 * c, 2 * c), 0) >= c)
            == (lax.broadcasted_iota(jnp.int32, (2 * c, 2 * c), 1) >= c))
    bd_f = ((lax.broadcasted_iota(jnp.int32, (PAIR, PAIR), 0) >= HEAD_DIM)
            == (lax.broadcasted_iota(jnp.int32, (PAIR, PAIR), 1) >= HEAD_DIM))
    zeros = jnp.zeros((c, PAIR), BF16)

    items = [(n, pr) for n in range(nc) for pr in range(N_PAIRS)]
    cut = lambda x, it: x[it[0] * c:(it[0] + 1) * c, it[1] * PAIR:(it[1] + 1) * PAIR]
    sel = lambda m, x: jnp.where(m, x, jnp.zeros_like(x))
    cat0 = lambda *xs: jnp.concatenate(xs, axis=0)
    cat1 = lambda *xs: jnp.concatenate(xs, axis=1)

    def block_diag(x):
        xb = x.astype(BF16)
        return sel(bd_c, cat0(xb, xb))

    at_i = [cut(at, it).astype(BF16) for it in items]
    rt_i = [cut(rt, it) for it in items]
    v_i = [cut(v, it).astype(BF16) for it in items]
    bk = [cat0(cut(bt, it), cut(kt, it)).astype(BF16) for it in items]
    kb = [cat0(cut(kt, it), cut(bt, it)).astype(BF16) for it in items]
    rtb = [x.astype(BF16) for x in rt_i]
    yield
    aa0 = [_dot_nt(cat0(sel(m0, a_), sel(m0, r_)), y_) for a_, r_, y_ in zip(at_i, rtb, bk)]
    yield
    aa1 = [_dot_nt(cat0(sel(m1, a_), sel(m1, r_)), y_) for a_, r_, y_ in zip(at_i, rtb, kb)]
    yield
    x = [jnp.where(strict, jnp.where(left, p0[:c], p1[:c]), 0.0) for p0, p1 in zip(aa0, aa1)]
    akp = [jnp.where(strict, jnp.where(left, p1[:c], p0[:c]), 0.0) for p0, p1 in zip(aa0, aa1)]
    arb = [jnp.where(incl, jnp.where(left, p0[c:], p1[c:]), 0.0) for p0, p1 in zip(aa0, aa1)]
    ark = [jnp.where(incl, jnp.where(left, p1[c:], p0[c:]), 0.0) for p0, p1 in zip(aa0, aa1)]

    tp = x
    n = c
    while n > 2:
        xd = [block_diag(xi) for xi in x]
        x = [_dot(xi, d) for xi, d in zip(x, xd)]
        yield
        xd = [block_diag(xi) for xi in x]
        tp = [t + xi + _dot(t, d) for t, xi, d in zip(tp, x, xd)]
        yield
        n //= 2

    akv = [_dot(m_, cat0(sel(m1, v_), sel(m0, v_))) for m_, v_ in zip(akp, v_i)]
    yield
    rhs2 = [cat1(cut(at, it), q_) for it, q_ in zip(items, akv)]
    split = lambda x_: cat0(sel(m0w, x_), sel(~m0w, x_))
    pq = [x_ + _dot(t, split(x_.astype(BF16))) for x_, t in zip(rhs2, tp)]
    yield
    gy = [_dot(cat1(b_, k_), cat0(split(x_.astype(BF16)), cat1(zeros, sel(m1, v_)), cat1(zeros, sel(m0, v_))))
          for b_, k_, x_, v_ in zip(arb, ark, pq, v_i)]
    yield
    pg = [cat0(x_[:, :PAIR], r_ + g_[:, :PAIR]) for x_, r_, g_ in zip(pq, rt_i, gy)]

    return [dict(pg=pg_, q=x_[:, PAIR:], y0=g_[:, PAIR:], v=v_, bkw=cat0(cut(bw, it), cut(kw, it)),
                 w_end=jnp.exp(cut(last, it)[:1]))
            for pg_, x_, g_, v_, it in zip(pg, pq, gy, v_i, items)]


def _rwkv_serial(chunks, states, c, per_seq):
    bd_f = ((lax.broadcasted_iota(jnp.int32, (PAIR, PAIR), 0) >= HEAD_DIM)
            == (lax.broadcasted_iota(jnp.int32, (PAIR, PAIR), 1) >= HEAD_DIM))
    states = list(states)
    n_chunks = len(chunks) // N_PAIRS
    y_rows = [None] * n_chunks
    for j in range(per_seq):
        which = [sq * per_seq + j for sq in range(n_chunks // per_seq)]
        data = [d for n in which for d in chunks[n * N_PAIRS:(n + 1) * N_PAIRS]]
        uy = [_dot_nt(d["pg"], s) for d, s in zip(data, states)]
        u = [uy_[:c] + d["q"] for uy_, d in zip(uy, data)]
        ys = [uy_[c:] + d["y0"] for uy_, d in zip(uy, data)]
        for i, n in enumerate(which):
            y_rows[n] = jnp.concatenate(ys[i * N_PAIRS:(i + 1) * N_PAIRS], axis=1)
        yield
        upd = [_dot_tn(jnp.concatenate([u_.astype(BF16), d["v"]], axis=0), d["bkw"]) for u_, d in zip(u, data)]
        states = [s * d["w_end"] + jnp.where(bd_f, x, 0.0) for s, d, x in zip(states, data, upd)]
        yield
    return jnp.concatenate(y_rows, axis=0), states


def _interleave(*gens):
    results = [None] * len(gens)
    live = list(range(len(gens)))
    while live:
        for i in list(live):
            try:
                next(gens[i])
            except StopIteration as done:
                results[i] = done.value
                live.remove(i)
    return results


def _rwkv_kernel(p_ref, shift0_ref, s0_ref, mu_ref, w0_ref, wl_ref, a0_ref, al_ref, gl_ref,
                 kk_ref, ka_ref, rk_ref, lng_ref, lnb_ref, bd1_ref, bdm_ref,
                 y_ref, s_ref, carry, *, c, groups):
    ci = pl.program_id(1)
    seqs, tile = p_ref.shape[0], p_ref.shape[1]
    per = seqs // groups
    tt = per * tile
    bd1 = bd1_ref[...]

    @pl.when(ci == 0)
    def _():
        carry[...] = shift0_ref[...]
        s_ref[...] = s0_ref[...]

    def prepare(first):
        p = jnp.concatenate([p_ref[first + sq] for sq in range(per)], axis=0)
        row = lax.broadcasted_iota(jnp.int32, (tt, 1), 0)
        prev = pltpu.roll(p, 1, 0)
        for sq in range(per):
            prev = jnp.where(row == sq * tile, carry[first + sq], prev)
            carry[first + sq] = p[(sq + 1) * tile - 1:(sq + 1) * tile, :]
        xs = p + (prev - p) * mu_ref[...]
        yield
        r = xs[:, :B_W]
        k = xs[:, B_W:2 * B_W]
        v = xs[:, 2 * B_W:3 * B_W]
        xwa = xs[:, 3 * B_W:3 * B_W + LANES]
        xg = xs[:, 3 * B_W + LANES:]
        w_log = -_softplus(-(w0_ref[...] + _dot(jnp.tanh(xwa), wl_ref[...]))) - 0.5
        logw = -jnp.exp(w_log)
        yield
        a = jax.nn.sigmoid(a0_ref[...] + _dot(xwa, al_ref[...]))
        yield
        g = _dot(jax.nn.sigmoid(xg), gl_ref[...])
        yield
        kk = k * kk_ref[...]
        kk = kk / jnp.maximum(jnp.sqrt(_group_reduce(kk * kk, bd1)), 1e-12)
        yield
        k = k * (1.0 + (a - 1.0) * ka_ref[...])

        grp = tile if tile >= LANES else tt
        row_t = lax.broadcasted_iota(jnp.int32, (grp, grp), 0)
        col_t = lax.broadcasted_iota(jnp.int32, (grp, grp), 1)
        tri = jnp.where((col_t <= row_t) & (col_t >= (row_t // c) * c), 1.0, 0.0).astype(BF16)
        parts = _split3(logw)
        cum = jnp.concatenate(
            [sum(jnp.dot(tri, x[i * grp:(i + 1) * grp], preferred_element_type=F32) for x in parts)
             for i in range(tt // grp)], axis=0)
        yield
        cum3 = cum.reshape(tt // c, c, B_W)
        last = jnp.broadcast_to(cum3[:, c - 1:c, :], cum3.shape).reshape(tt, B_W)
        return (r, k, v, kk, a, cum, logw, last), (r, k, v, g)

    bdm = bdm_ref[...]

    def finish(gi, y, states, r, k, v, g):
        for sq in range(per):
            for pr in range(N_PAIRS):
                s_ref[gi * per + sq, pr] = states[sq * N_PAIRS + pr]
        dev = y - _group_reduce(y, bdm)
        yn = dev * lax.rsqrt(_group_reduce(dev * dev, bdm) + GN_EPS) * lng_ref[...] + lnb_ref[...]
        bonus = _group_reduce(r * k * rk_ref[...], bd1) * v
        out = (yn + bonus) * g
        for sq in range(per):
            y_ref[gi * per + sq] = out[sq * tile:(sq + 1) * tile].astype(BF16)

    inputs, chunks, extras = {}, {}, {}
    for t in range(groups + 2):
        stages, tags = [], []
        if t < groups:
            stages.append(prepare(t * per))
            tags.append(("prepare", t))
        if 0 <= t - 1 < groups:
            stages.append(_rwkv_state_free(*inputs[t - 1], c))
            tags.append(("free", t - 1))
        if 0 <= t - 2 < groups:
            gi = t - 2
            states = [s_ref[gi * per + sq, pr] for sq in range(per) for pr in range(N_PAIRS)]
            stages.append(_rwkv_serial(chunks[gi], states, c, tile // c))
            tags.append(("serial", gi))
        for (kind, gi), res in zip(tags, _interleave(*stages)):
            if kind == "prepare":
                inputs[gi], extras[gi] = res
            elif kind == "free":
                chunks[gi] = res
            else:
                finish(gi, *res, *extras[gi])


def _rwkv(p, shift0, s0, weights, chunk, tile, seqs, groups):
    b, t, _ = p.shape
    seqs = math.gcd(seqs, b)
    wspecs = [_full(w.shape) for w in weights]
    return pl.pallas_call(
        functools.partial(_rwkv_kernel, c=chunk, groups=math.gcd(groups, seqs)),
        grid=(b // seqs, t // tile),
        in_specs=[pl.BlockSpec((seqs, tile, P_PAD), lambda bi, ci: (bi, ci, 0)),
                  pl.BlockSpec((seqs, 1, P_PAD), lambda bi, ci: (bi, 0, 0)),
                  pl.BlockSpec((seqs, N_PAIRS, PAIR, PAIR), lambda bi, ci: (bi, 0, 0, 0))] + wspecs,
        out_specs=[pl.BlockSpec((seqs, tile, B_W), lambda bi, ci: (bi, ci, 0)),
                   pl.BlockSpec((seqs, N_PAIRS, PAIR, PAIR), lambda bi, ci: (bi, 0, 0, 0))],
        out_shape=[jax.ShapeDtypeStruct((b, t, B_W), BF16),
                   jax.ShapeDtypeStruct((b, N_PAIRS, PAIR, PAIR), F32)],
        scratch_shapes=[pltpu.VMEM((seqs, 1, P_PAD), F32)],
        compiler_params=_params(("parallel", "arbitrary")),
        name="rwkv7",
    )(p, shift0, s0, *weights)


def _pair_states(s):
    b = s.shape[0]
    s = s.reshape(b, N_PAIRS, 2, HEAD_DIM, HEAD_DIM)
    z = jnp.zeros_like(s[:, :, 0])
    top = jnp.concatenate([s[:, :, 0], z], axis=-1)
    bot = jnp.concatenate([z, s[:, :, 1]], axis=-1)
    return jnp.concatenate([top, bot], axis=-2)


def _unpair_states(s):
    return jnp.stack([s[:, :, :HEAD_DIM, :HEAD_DIM], s[:, :, HEAD_DIM:, HEAD_DIM:]], axis=2).reshape(
        s.shape[0], H_RWKV, HEAD_DIM, HEAD_DIM)


def _outmlp_kernel(*refs, dilations):
    n_branch = len(dilations)
    x_ref = refs[0]
    o_refs = refs[1:1 + n_branch]
    lse_refs = refs[1 + n_branch:1 + 2 * n_branch] if n_branch > 1 else ()
    n_in = 1 + n_branch + len(lse_refs)
    rw_ref, wo_ref, g2_ref, w1_ref, w2_ref, y_ref = refs[n_in:n_in + 6]
    stage = refs[n_in + 6:]
    tm = x_ref.shape[0]

    def natural(ref, d, st):
        if d == 1:
            return ref[...].astype(F32)
        for r in range(d):
            for j in range(A_W // LANES):
                lo = r * A_W + j * LANES
                st[j, pl.ds(r, tm // d, stride=d), :] = ref[:, lo:lo + LANES].astype(F32)
        return jnp.concatenate([st[j] for j in range(A_W // LANES)], axis=1)

    if n_branch > 1:
        lses = [natural(ref, d, stage[2 * i]) for i, (ref, d) in enumerate(zip(lse_refs, dilations))]
        top = functools.reduce(jnp.maximum, lses)
        num = den = None
        for i, (o_ref, d, lse) in enumerate(zip(o_refs, dilations, lses)):
            e = jnp.exp(lse - top)
            t = e * natural(o_ref, d, stage[2 * i + 1])
            num = t if num is None else num + t
            den = e if den is None else den + e
        att = num / den
    else:
        att = o_refs[0][...]
    h = x_ref[...] + _dot(att, wo_ref[:A_W, :]) + jnp.dot(rw_ref[...], wo_ref[A_W:, :],
                                                           preferred_element_type=F32)
    ms = jnp.mean(h * h, axis=-1, keepdims=True)
    m = (h * lax.rsqrt(ms + RMS_EPS) * g2_ref[...]).astype(BF16)
    acc = None
    for c in range(D_FF // D_MODEL):
        sl = slice(c * D_MODEL, (c + 1) * D_MODEL)
        u = jnp.maximum(jnp.dot(m, w1_ref[:, sl], preferred_element_type=F32), 0.0)
        t = jnp.dot((u * u).astype(BF16), w2_ref[sl, :], preferred_element_type=F32)
        acc = t if acc is None else acc + t
    y_ref[...] = h + acc


def _outmlp(x, os_, lses, dilations, rw, wo, g2, w1, w2, tm):
    n_tok = x.shape[0]
    tm = min(tm, n_tok)
    row = lambda w: pl.BlockSpec((tm, w), lambda i: (i, 0))
    view = lambda d: pl.BlockSpec((tm // d, d * A_W), lambda i: (i, 0))
    const = lambda shape: pl.BlockSpec(shape, lambda i: (0, 0), pipeline_mode=pl.Buffered(1))
    n_stage = 2 * len(dilations) if len(dilations) > 1 else 0
    return pl.pallas_call(
        functools.partial(_outmlp_kernel, dilations=dilations),
        grid=(n_tok // tm,),
        in_specs=[row(D_MODEL)] + [view(d) for d in dilations] + [view(d) for d in dilations[:len(lses)]]
        + [row(B_W), const((D_MODEL, D_MODEL)), const((1, D_MODEL)), const((D_MODEL, D_FF)),
           const((D_FF, D_MODEL))],
        out_specs=row(D_MODEL),
        out_shape=jax.ShapeDtypeStruct((n_tok, D_MODEL), F32),
        scratch_shapes=[pltpu.VMEM((A_W // LANES, tm, LANES), F32)] * n_stage,
        compiler_params=_params(("parallel",)),
        name="outmlp",
    )(x, *os_, *lses, rw, wo, g2, w1, w2)


def _row(v, width=None):
    v = v.reshape(1, -1).astype(F32)
    if width is not None and v.shape[1] < width:
        v = jnp.pad(v, ((0, 0), (0, width - v.shape[1])))
    return v


def _tile_heads(g, n_heads):
    return jnp.tile(g.reshape(1, HEAD_DIM).astype(F32), (1, n_heads))


def kernel(x_prompt, x_sample, cache_k_win, cache_v_win, state_wkv, state_shift, bias_table, ln1_g, w_in,
           q_norm_g, k_norm_g, mu_shift, w0, w_lora2, a0, a_lora2, g_lora2, k_k, k_a, r_k, lnx_g, lnx_b,
           w_out, ln2_g, w_mlp1, w_mlp2):
    depth = w_in.shape[0]
    assert depth == 1, "a deeper stack would repeat the per-layer calls below"
    li = 0
    bp, tp, _ = x_prompt.shape
    bs, ts, _ = x_sample.shape
    keep = min(MAX_WINDOW, tp)

    wqkv = w_in[li][:, :3 * A_W].astype(BF16)
    wp = jnp.pad(w_in[li][:, 3 * A_W:], ((0, 0), (0, P_PAD - N_SHIFT))).astype(BF16)
    gq = _tile_heads(q_norm_g[li], H_ATTN)
    gk = _tile_heads(k_norm_g[li], H_ATTN)
    zrow = lambda n: jnp.zeros((n, B_W), F32)
    rwkv_weights = [
        _row(mu_shift[li], P_PAD), _row(w0[li]),
        jnp.concatenate([w_lora2[li].astype(F32), zrow(LANES - LORA_W)], axis=0).astype(BF16),
        _row(a0[li]),
        jnp.concatenate([zrow(LORA_W), a_lora2[li].astype(F32)], axis=0).astype(BF16),
        jnp.concatenate([g_lora2[li].astype(F32), zrow(LORA_PAD - LORA_G)], axis=0).astype(BF16),
        _row(k_k[li]), _row(k_a[li]), _row(r_k[li]), _row(lnx_g[li]), _row(lnx_b[li]),
        _block_diag(B_W, 1.0), _block_diag(B_W, 1.0 / HEAD_DIM),
    ]
    wo = w_out[li].astype(BF16)
    w1 = w_mlp1[li].astype(BF16)
    w2 = w_mlp2[li].astype(BF16)
    g1 = _row(ln1_g[li])
    g2 = _row(ln2_g[li])

    xp = x_prompt.reshape(bp * tp, D_MODEL)
    dils = tuple(d for _, d in BRANCHES)
    q, k, v, kf, vf, p, *views = _inproj(xp, g1, wqkv, wp, gq, gk, tm=512, seq=tp, keep=keep,
                                         dilations=dils[1:], transpose_kv=True)
    views = [q, k, v] + views
    os_, lses = [], []
    for i, d in enumerate(dils):
        qd, kd, vd = (a.reshape(bp, tp // d, d * A_W) for a in views[3 * i:3 * i + 3])
        o, lse = _attn_branch(qd, kd, vd, _branch_bias(bias_table, d), d, tq=1024)
        os_.append(o.reshape(bp * tp // d, d * A_W))
        lses.append(lse.reshape(bp * tp // d, d * A_W))
    p3 = p.reshape(bp, tp, P_PAD)
    rw, s_p = _rwkv(p3, jnp.zeros((bp, 1, P_PAD), F32), jnp.zeros((bp, N_PAIRS, PAIR, PAIR), F32),
                    rwkv_weights, chunk=64, tile=256, seqs=4, groups=1)
    y_p = _outmlp(xp, os_, lses, dils, rw.reshape(bp * tp, B_W), wo, g2, w1, w2, tm=512)
    heads = lambda a, b_, t_: a.reshape(b_, t_, H_ATTN, HEAD_DIM)
    window = lambda a: a.reshape(bp, H_ATTN, HEAD_DIM, keep).transpose(0, 3, 1, 2)[None]
    k_win_p = window(kf)
    v_win_p = window(vf)
    wkv_p = _unpair_states(s_p)[None]
    shift_p = p3[:, -1, :N_SHIFT][None]

    xs = x_sample.reshape(bs * ts, D_MODEL)
    qs, _, _, kfs, vfs, ps = _inproj(xs, g1, wqkv, wp, gq, gk, tm=512)
    head_major = lambda a: a.reshape(bs, ts, H_ATTN, HEAD_DIM).transpose(0, 2, 1, 3)
    cache_t = lambda c: c[li].transpose(0, 2, 3, 1)
    att_s = _sample_attn(head_major(qs.astype(F32)), head_major(kfs), head_major(vfs),
                         cache_t(cache_k_win), cache_t(cache_v_win), bias_table)
    att_s = att_s.transpose(0, 2, 1, 3)
    ps3 = ps.reshape(bs, ts, P_PAD)
    shift0 = jnp.pad(state_shift[li].astype(F32), ((0, 0), (0, P_PAD - N_SHIFT))).reshape(bs, 1, P_PAD)
    rw_s, s_s = _rwkv(ps3, shift0, _pair_states(state_wkv[li].astype(F32)), rwkv_weights, chunk=ts, tile=ts,
                      seqs=8, groups=1)
    y_s = _outmlp(xs, [att_s.reshape(bs * ts, A_W)], [], (1,), rw_s.reshape(bs * ts, B_W), wo, g2, w1, w2, tm=256)

    return (y_p.reshape(bp, tp, D_MODEL), y_s.reshape(bs, ts, D_MODEL), k_win_p, v_win_p, wkv_p, shift_p,
            heads(kfs, bs, ts)[None], heads(vfs, bs, ts)[None], _unpair_states(s_s)[None],
            ps3[:, -1, :N_SHIFT][None])
```
